```python
import math
import jax, jax.numpy as jnp
from jax import lax
import numpy as np

D_MODEL = 4096
BATCH = 4
SEQ = 4096
DEPTH = 4

GRID_W = 64
CTX_LEN = 256
N_MIXERS = 3
MIXER_POOL = 0
MIXER_CONV = 1
MIXER_GDN = 2
POOL_WINDOWS = (2, 4, 8, 16)
N_POOL_GROUPS = len(POOL_WINDOWS)
POOL_GROUP = D_MODEL // N_POOL_GROUPS
D_FF = 3 * D_MODEL // 2
CONV_W = 3
GDN_HEAD_DIM = 128
GDN_HEADS = D_MODEL // GDN_HEAD_DIM
GDN_CHUNK = 64
N_DIRS = 2
GDN_PROJ = 4 * D_MODEL + 2 * N_DIRS * GDN_HEADS
NORM_EPS = 1e-6
MOD_STD = 0.5

kernel_name = 'hybrid_pool_conv_gdn_prefix_dit'


def _n_of_kind(kind):
    return len(range(kind, DEPTH, N_MIXERS))


def _rmsnorm(x, g):
    xf = x.astype(jnp.float32)
    y = xf * lax.rsqrt(jnp.mean(xf * xf, axis=-1, keepdims=True) + NORM_EPS)
    return (y * g.astype(jnp.float32)).astype(x.dtype)


def _modulate(h, shift, scale):
    return h * (1 + scale) + shift


def _dwconv(x, w):
    t = x.shape[1]
    p = CONV_W // 2
    xp = jnp.pad(x, ((0, 0), (p, CONV_W - 1 - p), (0, 0)))
    return sum(xp[:, k:k + t] * w[k] for k in range(CONV_W))


def _box_mean(x, w, axis):
    n = x.shape[axis]
    pos = jnp.arange(n)
    lo = jnp.clip(pos - w // 2, 0, n)
    hi = jnp.clip(pos + (w - w // 2), 0, n)
    pad = [(0, 0)] * x.ndim
    pad[axis] = (1, 0)
    cs = jnp.pad(jnp.cumsum(x, axis=axis), pad)
    shape = [1] * x.ndim
    shape[axis] = n
    cnt = (hi - lo).astype(x.dtype).reshape(shape)
    return (jnp.take(cs, hi, axis=axis) - jnp.take(cs, lo, axis=axis)) / cnt


def _pool_mixer(h, pool_w, pool_scale, rows):
    b, t, d = h.shape
    hg = h.reshape(b, t, N_POOL_GROUPS, POOL_GROUP)
    diffs = []
    for gi, w in enumerate(POOL_WINDOWS):
        xg = hg[:, :, gi, :].astype(jnp.float32)
        if rows is None:
            m = _box_mean(xg, w, 1)
        else:
            x2 = xg.reshape(b, rows, GRID_W, POOL_GROUP)
            m = _box_mean(_box_mean(x2, w, 1), w, 2).reshape(b, t, POOL_GROUP)
        diffs.append(m - xg)
    dlt = jnp.stack(diffs, axis=2).astype(h.dtype)
    y = jnp.einsum('btgi,gio->btgo', dlt, pool_w).reshape(b, t, d)
    return y * pool_scale


def _shortconv_mixer(h, w_in, conv_w, w_out):
    bg, cg, v = jnp.split(h @ w_in, 3, axis=-1)
    return (bg * _dwconv(cg * v, conv_w)) @ w_out


def _conv_ffn(h, w_in, conv_w, w_out):
    a, u = jnp.split(h @ w_in, 2, axis=-1)
    return (jax.nn.silu(_dwconv(a, conv_w)) * u) @ w_out


def _l2norm(x):
    return x * lax.rsqrt(jnp.sum(x * x, axis=-1, keepdims=True) + NORM_EPS)


def _gdn_inputs(h, w_in, conv_w, a_log, dt_bias):
    b, t, _ = h.shape
    proj = h @ w_in
    d3, d4, d5 = 3 * D_MODEL, 4 * D_MODEL, 4 * D_MODEL + N_DIRS * GDN_HEADS
    qkv, gate, a, bl = proj[..., :d3], proj[..., d3:d4], proj[..., d4:d5], proj[..., d5:]
    qkv = jax.nn.silu(_dwconv(qkv, conv_w)).astype(jnp.float32)
    q, k, v = jnp.split(qkv.reshape(b, t, 3, GDN_HEADS, GDN_HEAD_DIM), 3, axis=2)
    q = _l2norm(q[:, :, 0]) * (GDN_HEAD_DIM ** -0.5)
    k = _l2norm(k[:, :, 0])
    v = v[:, :, 0]
    a = a.astype(jnp.float32).reshape(b, t, N_DIRS, GDN_HEADS)
    log_alpha = -jnp.exp(a_log.astype(jnp.float32)) * jax.nn.softplus(a + dt_bias.astype(jnp.float32))
    beta = jax.nn.sigmoid(bl.astype(jnp.float32).reshape(b, t, N_DIRS, GDN_HEADS))
    return q, k, v, gate, log_alpha, beta


def _chunk_gated_delta(q, k, v, log_alpha, beta, s0):
    b, t, h, dk = q.shape
    dv = v.shape[-1]
    n = t // GDN_CHUNK

    def chunk(z):
        return z.reshape(b, n, GDN_CHUNK, h, -1).transpose(0, 3, 1, 2, 4)

    q, k, v = chunk(q), chunk(k), chunk(v)
    g = jnp.cumsum(log_alpha.reshape(b, n, GDN_CHUNK, h).transpose(0, 3, 1, 2), axis=-1)
    beta = beta.reshape(b, n, GDN_CHUNK, h).transpose(0, 3, 1, 2)
    causal = jnp.tril(jnp.ones((GDN_CHUNK, GDN_CHUNK), dtype=bool))
    gamma = jnp.exp(jnp.where(causal, g[..., :, None] - g[..., None, :], -jnp.inf))
    kb = k * beta[..., None]
    a_mat = jnp.einsum('bhnid,bhnjd->bhnij', kb, k) * gamma
    eg = jnp.exp(g)[..., None]
    value = lax.linalg.triangular_solve(a_mat, v * beta[..., None], left_side=True, lower=True, unit_diagonal=True)
    kcum = lax.linalg.triangular_solve(a_mat, kb * eg, left_side=True, lower=True, unit_diagonal=True)
    qk = jnp.einsum('bhnid,bhnjd->bhnij', q, k) * gamma
    q_dec = q * eg
    g_last = g[..., -1:]
    k_dec = k * jnp.exp(g_last - g)[..., None]
    decay = jnp.exp(g_last[..., 0])

    def step(s, xs):
        qk_n, value_n, kcum_n, qd_n, kd_n, dec_n = xs
        v_new = value_n - jnp.einsum('bhck,bhkv->bhcv', kcum_n, s)
        o_n = jnp.einsum('bhck,bhkv->bhcv', qd_n, s) + jnp.einsum('bhij,bhjv->bhiv', qk_n, v_new)
        s = s * dec_n[..., None, None] + jnp.einsum('bhck,bhcv->bhkv', kd_n, v_new)
        return s, o_n

    xs = tuple(jnp.moveaxis(z, 2, 0) for z in (qk, value, kcum, q_dec, k_dec, decay))
    s_final, o = lax.scan(step, s0.astype(jnp.float32), xs)
    o = o.transpose(1, 0, 3, 2, 4).reshape(b, t, h, dv)
    return o, s_final


def _flip(z, rev):
    return jnp.flip(z, axis=1) if rev else z


def _gdn_out(o, gate, norm_g, w_out, dtype):
    b, t = o.shape[:2]
    on = o * lax.rsqrt(jnp.mean(o * o, axis=-1, keepdims=True) + NORM_EPS) * norm_g.astype(jnp.float32)
    gt = jax.nn.silu(gate.astype(jnp.float32).reshape(b, t, GDN_HEADS, GDN_HEAD_DIM))
    return (on * gt).reshape(b, t, D_MODEL).astype(dtype) @ w_out


def _gdn_mixer(h, hc, w_in, conv_w, a_log, dt_bias, norm_g, w_out, ctx_out):
    q, k, v, gate, la, be = _gdn_inputs(h, w_in, conv_w, a_log, dt_bias)
    qc, kc, vc, gatec, lac, bec = _gdn_inputs(hc, w_in, conv_w, a_log, dt_bias)
    s0 = jnp.zeros((h.shape[0], GDN_HEADS, GDN_HEAD_DIM, GDN_HEAD_DIM), jnp.float32)
    o_lat, o_ctx = [], []
    for d in range(N_DIRS):
        rev = d == 1
        oc, sc = _chunk_gated_delta(_flip(qc, rev), _flip(kc, rev), _flip(vc, rev),
                                    _flip(lac[:, :, d], rev), _flip(bec[:, :, d], rev), s0)
        ol, _ = _chunk_gated_delta(_flip(q, rev), _flip(k, rev), _flip(v, rev),
                                   _flip(la[:, :, d], rev), _flip(be[:, :, d], rev), sc)
        o_lat.append(_flip(ol, rev))
        o_ctx.append(_flip(oc, rev))
    y = _gdn_out(o_lat[0] + o_lat[1], gate, norm_g, w_out, h.dtype)
    yc = _gdn_out(o_ctx[0] + o_ctx[1], gatec, norm_g, w_out, hc.dtype) if ctx_out else None
    return y, yc


def setup_inputs(seed: int = 0) -> dict:
    key = jax.random.key(seed)
    ks = iter(jax.random.split(key, 32))
    f32 = jnp.float32

    def nrm(shape, std):
        return jax.random.normal(next(ks), shape, f32) * std

    def gain(shape):
        return 1.0 + nrm(shape, 0.02)

    n_a, n_b, n_c = _n_of_kind(MIXER_POOL), _n_of_kind(MIXER_CONV), _n_of_kind(MIXER_GDN)
    a_log = jnp.log(jax.random.uniform(next(ks), (n_c, N_DIRS, GDN_HEADS), f32, 1.0, 16.0))
    dt = jnp.exp(jax.random.uniform(next(ks), (n_c, N_DIRS, GDN_HEADS), f32, math.log(1e-3), math.log(1e-1)))
    dt_bias = dt + jnp.log(-jnp.expm1(-dt))
    return {
        'x': nrm((BATCH, SEQ, D_MODEL), 1.0),
        'c': nrm((BATCH, D_MODEL), 1.0),
        'ctx': nrm((BATCH, CTX_LEN, D_MODEL), 1.0),
        'c_ctx': nrm((D_MODEL,), 1.0),
        'mod_w': nrm((DEPTH, D_MODEL, 6 * D_MODEL), MOD_STD * D_MODEL ** -0.5),
        'mod_b': nrm((DEPTH, 6 * D_MODEL), 0.02),
        'norm1_g': gain((DEPTH, D_MODEL)),
        'norm2_g': gain((DEPTH, D_MODEL)),
        'ffn_w_in': nrm((DEPTH, D_MODEL, 2 * D_FF), D_MODEL ** -0.5),
        'ffn_conv': nrm((DEPTH, CONV_W, D_FF), CONV_W ** -0.5),
        'ffn_w_out': nrm((DEPTH, D_FF, D_MODEL), D_FF ** -0.5),
        'pool_w': nrm((n_a, N_POOL_GROUPS, POOL_GROUP, POOL_GROUP), POOL_GROUP ** -0.5),
        'pool_scale': 1.0 + nrm((n_a, D_MODEL), 0.1),
        'sc_w_in': nrm((n_b, D_MODEL, 3 * D_MODEL), D_MODEL ** -0.5),
        'sc_conv': nrm((n_b, CONV_W, D_MODEL), CONV_W ** -0.5),
        'sc_w_out': nrm((n_b, D_MODEL, D_MODEL), D_MODEL ** -0.5),
        'gdn_w_in': nrm((n_c, D_MODEL, GDN_PROJ), D_MODEL ** -0.5),
        'gdn_conv': nrm((n_c, CONV_W, 3 * D_MODEL), CONV_W ** -0.5),
        'gdn_a_log': a_log,
        'gdn_dt_bias': dt_bias,
        'gdn_norm_g': gain((n_c, GDN_HEAD_DIM)),
        'gdn_w_out': nrm((n_c, D_MODEL, D_MODEL), D_MODEL ** -0.5),
        'final_g': gain((D_MODEL,)),
    }


def reference(x, c, ctx, c_ctx, mod_w, mod_b, norm1_g, norm2_g, ffn_w_in, ffn_conv, ffn_w_out,
              pool_w, pool_scale, sc_w_in, sc_conv, sc_w_out, gdn_w_in, gdn_conv, gdn_a_log,
              gdn_dt_bias, gdn_norm_g, gdn_w_out, final_g):
    rows = x.shape[1] // GRID_W
    gdn_layers = [i for i in range(DEPTH) if i % N_MIXERS == MIXER_GDN]
    last_ctx = gdn_layers[-1] if gdn_layers else -1
    silu_c = jax.nn.silu(c)
    silu_cc = jax.nn.silu(c_ctx)
    y, yc = x, ctx
    for i in range(DEPTH):
        kind, j = i % N_MIXERS, i // N_MIXERS
        use_ctx = i <= last_ctx
        ctx_stream = i < last_ctx
        sh1, sc1, g1, sh2, sc2, g2 = jnp.split((silu_c @ mod_w[i] + mod_b[i])[:, None, :], 6, axis=-1)
        h = _modulate(_rmsnorm(y, norm1_g[i]), sh1, sc1)
        if use_ctx:
            csh1, csc1, cg1, csh2, csc2, cg2 = jnp.split(silu_cc @ mod_w[i] + mod_b[i], 6, axis=-1)
            hc = _modulate(_rmsnorm(yc, norm1_g[i]), csh1, csc1)
        if kind == MIXER_POOL:
            m = _pool_mixer(h, pool_w[j], pool_scale[j], rows)
            mc = _pool_mixer(hc, pool_w[j], pool_scale[j], None) if ctx_stream else None
        elif kind == MIXER_CONV:
            m = _shortconv_mixer(h, sc_w_in[j], sc_conv[j], sc_w_out[j])
            mc = _shortconv_mixer(hc, sc_w_in[j], sc_conv[j], sc_w_out[j]) if ctx_stream else None
        else:
            m, mc = _gdn_mixer(h, hc, gdn_w_in[j], gdn_conv[j], gdn_a_log[j], gdn_dt_bias[j],
                               gdn_norm_g[j], gdn_w_out[j], ctx_stream)
        y = y + g1 * m
        y = y + g2 * _conv_ffn(_modulate(_rmsnorm(y, norm2_g[i]), sh2, sc2), ffn_w_in[i], ffn_conv[i], ffn_w_out[i])
        if ctx_stream:
            yc = yc + cg1 * mc
            yc = yc + cg2 * _conv_ffn(_modulate(_rmsnorm(yc, norm2_g[i]), csh2, csc2),
                                      ffn_w_in[i], ffn_conv[i], ffn_w_out[i])
    return _rmsnorm(y, final_g)
```

```python
import functools
import math

import jax
import jax.numpy as jnp
from jax import lax
from jax.experimental import pallas as pl
from jax.experimental.pallas import tpu as pltpu

F32 = jnp.float32
BF16 = jnp.bfloat16

GRID_W = 64
POOL_WINDOWS = (2, 4, 8, 16)
HEAD_DIM = 128
N_DIRS = 2
NORM_EPS = 1e-6
SCAN_BLOCK = 128
HALO = 16
LANES = 128
VMEM_LIMIT_BYTES = 56 * 1024 * 1024
ROW_TILE = 512
COL_TILE = 512
POOL_COLS = 256
CORE_HEADS = 4


def _params(*sem):
    return pltpu.CompilerParams(dimension_semantics=sem, vmem_limit_bytes=VMEM_LIMIT_BYTES)


def _col_tile(n, cap=None):
    cap = COL_TILE if cap is None else cap
    for t in (cap, 512, 256, 128):
        if t <= cap and n % t == 0:
            return t
    raise ValueError(f"no column tile for {n}")


def _silu(x):
    return x * jax.nn.sigmoid(x)


def _rmsnorm(y, g):
    return y * lax.rsqrt(jnp.mean(y * y, axis=-1, keepdims=True) + NORM_EPS) * g


def _mod_kernel(c_ref, w_ref, b_ref, o_ref):
    s = _silu(c_ref[...]).astype(BF16)
    o_ref[...] = jnp.dot(s, w_ref[...].astype(BF16), preferred_element_type=F32) + b_ref[...]


def _mod_vectors(c, c_ctx, mod_w, mod_b):
    depth, d, d6 = mod_w.shape
    b = c.shape[0]
    r = -(-(b + 1) // 16) * 16
    cc = jnp.zeros((r, d), F32).at[:b].set(c).at[b].set(c_ctx)
    tn = _col_tile(d6)
    out = pl.pallas_call(
        _mod_kernel,
        grid=(depth, d6 // tn),
        in_specs=[pl.BlockSpec((r, d), lambda l, j: (0, 0)),
                  pl.BlockSpec((None, d, tn), lambda l, j: (l, 0, j)),
                  pl.BlockSpec((None, 1, tn), lambda l, j: (l, 0, j))],
        out_specs=pl.BlockSpec((None, r, tn), lambda l, j: (l, 0, j)),
        out_shape=jax.ShapeDtypeStruct((depth, r, d6), F32),
        compiler_params=_params("parallel", "parallel"),
        name="mod",
    )(cc, mod_w, mod_b.reshape(depth, 1, d6))
    return out.reshape(depth, r, 6, d)


class _Stream:
    def __init__(self, n_seq, seq_len, tm, ctx_row=None):
        assert seq_len % tm == 0 and tm % HALO == 0
        self.n_seq, self.seq_len, self.tm = n_seq, seq_len, tm
        self.tiles_per_seq = seq_len // tm
        self.rows = n_seq * seq_len
        self.n_tiles = self.rows // tm
        self.ctx_row = ctx_row

    def mod_row(self, i):
        return i // self.tiles_per_seq if self.ctx_row is None else self.ctx_row

    def mod_spec(self, layer, d, cols=None):
        if cols is None:
            return pl.BlockSpec((None, None, 6, d), lambda i, j: (layer, self.mod_row(i), 0, 0))
        return pl.BlockSpec((None, None, 6, cols), lambda i, j: (layer, self.mod_row(i), 0, j))


def _norm_kernel(y_ref, mod_ref, g_ref, o_ref):
    m = mod_ref[...]
    h = _rmsnorm(y_ref[...], g_ref[...]) * (1.0 + m[1:2]) + m[0:1]
    o_ref[...] = h.astype(o_ref.dtype)


def _norm_mod(y, modr, layer, g, st, out_dtype):
    rows, d = y.shape
    return pl.pallas_call(
        _norm_kernel,
        grid=(st.n_tiles, 1),
        in_specs=[pl.BlockSpec((st.tm, d), lambda i, j: (i, 0)),
                  st.mod_spec(layer, d),
                  pl.BlockSpec((1, d), lambda i, j: (0, 0))],
        out_specs=pl.BlockSpec((st.tm, d), lambda i, j: (i, 0)),
        out_shape=jax.ShapeDtypeStruct((rows, d), out_dtype),
        compiler_params=_params("parallel", "arbitrary"),
        name="norm_mod",
    )(y, modr, g.reshape(1, d))


def _window_sum(x, pos, n, unit, w):
    tot = x.shape[0]
    fwd, bwd = w - w // 2, w // 2
    assert fwd & (fwd - 1) == 0 and bwd & (bwd - 1) == 0 and bwd >= 1
    a, s = x, 1
    while s < fwd:
        a = a + jnp.where(pos + s < n, pltpu.roll(a, tot - s * unit, 0), 0.0)
        s *= 2
    b, s = jnp.where(pos >= 1, pltpu.roll(x, unit, 0), 0.0), 1
    while s < bwd:
        b = b + jnp.where(pos - s >= 0, pltpu.roll(b, s * unit, 0), 0.0)
        s *= 2
    return a + b


def _window_count(pos, n, w):
    return (jnp.minimum(pos + (w - w // 2), n) - jnp.maximum(pos - w // 2, 0)).astype(F32)


def _pool_kernel(h_ref, o_ref, *, tiles_per_group, grid_w, grid_rows):
    group = pl.program_id(1) // tiles_per_group
    t = h_ref.shape[0]
    tok = lax.broadcasted_iota(jnp.int32, (t, 1), 0)
    for gi, w in enumerate(POOL_WINDOWS):
        @pl.when(group == gi)
        def _(w=w):
            x = h_ref[...]
            if grid_rows is None:
                s = _window_sum(x, tok, t, 1, w)
                cnt = _window_count(tok, t, w)
            else:
                shift = grid_w.bit_length() - 1
                col, row = tok & (grid_w - 1), tok >> shift
                s = _window_sum(x, row, grid_rows, grid_w, w)
                s = _window_sum(s, col, grid_w, 1, w)
                cnt = _window_count(row, grid_rows, w) * _window_count(col, grid_w, w)
            o_ref[...] = (s / cnt - x).astype(o_ref.dtype)


def _pool_delta(h, n_seq, seq_len, grid_w):
    rows, d = h.shape
    group = d // len(POOL_WINDOWS)
    ct = min(POOL_COLS, group)
    assert group % ct == 0
    if grid_w is not None:
        assert grid_w & (grid_w - 1) == 0 and seq_len % grid_w == 0
    kern = functools.partial(_pool_kernel, tiles_per_group=group // ct, grid_w=grid_w,
                             grid_rows=None if grid_w is None else seq_len // grid_w)
    return pl.pallas_call(
        kern,
        grid=(n_seq, d // ct),
        in_specs=[pl.BlockSpec((seq_len, ct), lambda b, j: (b, j))],
        out_specs=pl.BlockSpec((seq_len, ct), lambda b, j: (b, j)),
        out_shape=jax.ShapeDtypeStruct((rows, d), BF16),
        compiler_params=_params("parallel", "parallel"),
        name="pool",
    )(h)


def _halo_specs(st, d):
    per = st.tm // HALO
    last = st.rows // HALO - 1
    return [pl.BlockSpec((st.tm, d), lambda i, j: (i, 0)),
            pl.BlockSpec((HALO, d), lambda i, j: (jnp.maximum(i * per - 1, 0), 0)),
            pl.BlockSpec((HALO, d), lambda i, j: (jnp.minimum((i + 1) * per, last), 0))]


def _fill_lhs(lhs_ref, hm_ref, hp_ref, hn_ref, tiles_per_seq):
    tm = hm_ref.shape[0]
    i = pl.program_id(0) % tiles_per_seq
    lhs_ref[0:tm, :] = hm_ref[...]
    r = lax.broadcasted_iota(jnp.int32, hp_ref.shape, 0)
    nxt = jnp.where(i == tiles_per_seq - 1, jnp.zeros_like(hn_ref[...]), hn_ref[...])
    prv = jnp.where(i == 0, jnp.zeros_like(hp_ref[...]), hp_ref[...])
    lhs_ref[tm:tm + HALO, :] = jnp.where(r < HALO // 2, nxt, prv)


def _conv3(a_ext, cw, tm):
    n = a_ext.shape[0]
    dn = pltpu.roll(a_ext, 1, 0)[0:tm]
    up = pltpu.roll(a_ext, n - 1, 0)[0:tm]
    return cw[0:1] * dn + cw[1:2] * a_ext[0:tm] + cw[2:3] * up


def _ffn_in_kernel(hm_ref, hp_ref, hn_ref, wa_ref, wu_ref, cw_ref, o_ref, lhs_ref, *, tiles_per_seq):
    tm = hm_ref.shape[0]

    @pl.when(pl.program_id(1) == 0)
    def _():
        _fill_lhs(lhs_ref, hm_ref, hp_ref, hn_ref, tiles_per_seq)

    a = jnp.dot(lhs_ref[...], wa_ref[...], preferred_element_type=F32)
    u = jnp.dot(lhs_ref[0:tm, :], wu_ref[...], preferred_element_type=F32)
    o_ref[...] = (_silu(_conv3(a, cw_ref[...], tm)) * u).astype(o_ref.dtype)


def _ffn_in(h, w_in, conv, st):
    rows, d = h.shape
    f = conv.shape[1]
    tn = _col_tile(f)
    nj = f // tn
    kern = functools.partial(_ffn_in_kernel, tiles_per_seq=st.tiles_per_seq)
    return pl.pallas_call(
        kern,
        grid=(st.n_tiles, nj),
        in_specs=_halo_specs(st, d) + [
            pl.BlockSpec((d, tn), lambda i, j: (0, j)),
            pl.BlockSpec((d, tn), lambda i, j: (0, j + nj)),
            pl.BlockSpec((3, tn), lambda i, j: (0, j))],
        out_specs=pl.BlockSpec((st.tm, tn), lambda i, j: (i, j)),
        out_shape=jax.ShapeDtypeStruct((rows, f), BF16),
        scratch_shapes=[pltpu.VMEM((st.tm + HALO, d), BF16)],
        compiler_params=_params("parallel", "arbitrary"),
        name="ffn_in",
    )(h, h, h, w_in, w_in, conv)


def _sc_in_kernel(hm_ref, hp_ref, hn_ref, wb_ref, wc_ref, wv_ref, cw_ref, o_ref, lhs_ref, *, tiles_per_seq):
    tm = hm_ref.shape[0]

    @pl.when(pl.program_id(1) == 0)
    def _():
        _fill_lhs(lhs_ref, hm_ref, hp_ref, hn_ref, tiles_per_seq)

    lhs = lhs_ref[...]
    cg = jnp.dot(lhs, wc_ref[...], preferred_element_type=F32)
    v = jnp.dot(lhs, wv_ref[...], preferred_element_type=F32)
    bg = jnp.dot(lhs_ref[0:tm, :], wb_ref[...], preferred_element_type=F32)
    o_ref[...] = (bg * _conv3(cg * v, cw_ref[...], tm)).astype(o_ref.dtype)


def _sc_in(h, w_in, conv, st):
    rows, d = h.shape
    n = conv.shape[1]
    tn = _col_tile(n)
    nj = n // tn
    kern = functools.partial(_sc_in_kernel, tiles_per_seq=st.tiles_per_seq)
    return pl.pallas_call(
        kern,
        grid=(st.n_tiles, nj),
        in_specs=_halo_specs(st, d) + [
            pl.BlockSpec((d, tn), lambda i, j: (0, j)),
            pl.BlockSpec((d, tn), lambda i, j: (0, j + nj)),
            pl.BlockSpec((d, tn), lambda i, j: (0, j + 2 * nj)),
            pl.BlockSpec((3, tn), lambda i, j: (0, j))],
        out_specs=pl.BlockSpec((st.tm, tn), lambda i, j: (i, j)),
        out_shape=jax.ShapeDtypeStruct((rows, n), BF16),
        scratch_shapes=[pltpu.VMEM((st.tm + HALO, d), BF16)],
        compiler_params=_params("parallel", "arbitrary"),
        name="sc_in",
    )(h, h, h, w_in, w_in, w_in, conv)


def _block_scan(x, pos, block, reverse):
    n = x.shape[0]
    s = 1
    while s < block:
        if reverse:
            x = x + jnp.where(pos + s < block, pltpu.roll(x, n - s, 0), 0.0)
        else:
            x = x + jnp.where(pos >= s, pltpu.roll(x, s, 0), 0.0)
        s *= 2
    return x


def _gdn_in_kernel(hm_ref, hp_ref, hn_ref, w_ref, wab_ref, cw_ref, ab_ref, o_ref, gb_ref, gbt_ref, lhs_ref,
                   *, tiles_per_seq, n_qk_tiles, n_conv_tiles, n_heads):
    tm = hm_ref.shape[0]
    j = pl.program_id(1)
    tn = w_ref.shape[1]

    @pl.when(j == 0)
    def _():
        _fill_lhs(lhs_ref, hm_ref, hp_ref, hn_ref, tiles_per_seq)
        ab = jnp.dot(lhs_ref[0:tm, :], wab_ref[...], preferred_element_type=F32)
        p = ab_ref[...]
        x = ab + p[1:2]
        log_alpha = -jnp.exp(p[0:1]) * (jnp.maximum(x, 0.0) + jnp.log1p(jnp.exp(-jnp.abs(x))))
        beta = jax.nn.sigmoid(ab)
        pos = lax.broadcasted_iota(jnp.int32, (tm, 1), 0) & (SCAN_BLOCK - 1)
        lane = lax.broadcasted_iota(jnp.int32, (1, LANES), 1)
        g = jnp.where(lane < n_heads, _block_scan(log_alpha, pos, SCAN_BLOCK, False),
                      _block_scan(log_alpha, pos, SCAN_BLOCK, True))
        gb = jnp.where(lane < N_DIRS * n_heads, g, beta)
        gb_ref[...] = gb
        gbt_ref[...] = gb.T

    a = jnp.dot(lhs_ref[...], w_ref[...], preferred_element_type=F32)

    def store(x):
        for hh in range(tn // HEAD_DIM):
            o_ref[hh] = x[:, hh * HEAD_DIM:(hh + 1) * HEAD_DIM].astype(o_ref.dtype)

    @pl.when(j < n_qk_tiles)
    def _():
        x = _silu(_conv3(a, cw_ref[...], tm))
        scale = jnp.where(j < n_qk_tiles // 2, HEAD_DIM ** -0.5, 1.0).astype(F32)
        for hh in range(tn // HEAD_DIM):
            xh = x[:, hh * HEAD_DIM:(hh + 1) * HEAD_DIM]
            xh = xh * lax.rsqrt(jnp.sum(xh * xh, axis=-1, keepdims=True) + NORM_EPS) * scale
            o_ref[hh] = xh.astype(o_ref.dtype)

    @pl.when(jnp.logical_and(j >= n_qk_tiles, j < n_conv_tiles))
    def _():
        store(_silu(_conv3(a, cw_ref[...], tm)))

    @pl.when(j >= n_conv_tiles)
    def _():
        store(a[0:tm])


def _gdn_in(h, w_main, w_ab, conv, ab_par, st, n_heads):
    rows, d = h.shape
    n = w_main.shape[1]
    tn = _col_tile(d)
    nj = n // tn
    n_conv = conv.shape[1] // tn
    kern = functools.partial(_gdn_in_kernel, tiles_per_seq=st.tiles_per_seq, n_qk_tiles=2 * d // tn,
                             n_conv_tiles=n_conv, n_heads=n_heads)
    hpt = tn // HEAD_DIM
    return pl.pallas_call(
        kern,
        grid=(st.n_tiles, nj),
        in_specs=_halo_specs(st, d) + [
            pl.BlockSpec((d, tn), lambda i, j: (0, j)),
            pl.BlockSpec((d, LANES), lambda i, j: (0, 0)),
            pl.BlockSpec((3, tn), lambda i, j: (0, jnp.minimum(j, n_conv - 1))),
            pl.BlockSpec((2, LANES), lambda i, j: (0, 0))],
        out_specs=[pl.BlockSpec((hpt, st.tm, HEAD_DIM), lambda i, j: (j, i, 0)),
                   pl.BlockSpec((st.tm, LANES), lambda i, j: (i, 0)),
                   pl.BlockSpec((LANES, st.tm), lambda i, j: (0, i))],
        out_shape=[jax.ShapeDtypeStruct((n // HEAD_DIM, rows, HEAD_DIM), BF16),
                   jax.ShapeDtypeStruct((rows, LANES), F32),
                   jax.ShapeDtypeStruct((LANES, rows), F32)],
        scratch_shapes=[pltpu.VMEM((st.tm + HALO, d), BF16)],
        compiler_params=_params("parallel", "arbitrary"),
        name="gdn_in",
    )(h, h, h, w_main, w_ab, conv, ab_par)


def _bdot(a, b):
    return jnp.dot(a.astype(BF16), b.astype(BF16), preferred_element_type=F32)


def _unit_tri_solve(neg_l, rhs, ri, ci):
    c = neg_l.shape[0]
    base = 8

    def same_block(width):
        sh = width.bit_length() - 1
        return (ri >> sh) == (ci >> sh)

    n0 = jnp.where(same_block(base), neg_l, 0.0)
    p2 = _bdot(n0, n0)
    t = _bdot(jnp.concatenate([n0, p2], axis=0), p2)
    x = n0 + p2 + t[0:c]
    p4 = t[c:2 * c]
    x = x + p4 + _bdot(x, p4)
    size = base
    while size < c:
        pair = jnp.logical_and(same_block(2 * size), jnp.logical_not(same_block(size)))
        cm = jnp.where(pair, neg_l, 0.0)
        a = cm + _bdot(x, cm)
        x = x + a + _bdot(a, x)
        size *= 2
    return rhs + _bdot(x, rhs)


def _delta_block(q, k, v, gcol, bcol, grow, s, reverse):
    c = q.shape[0]
    ri = lax.broadcasted_iota(jnp.int32, (c, c), 0)
    ci = lax.broadcasted_iota(jnp.int32, (c, c), 1)
    seen = (ri <= ci) if reverse else (ri >= ci)
    before = (ri < ci) if reverse else (ri > ci)
    g_end = gcol[0:1] if reverse else gcol[c - 1:c]
    kf = k.astype(F32)
    kb = kf * bcol
    eg = jnp.exp(gcol)
    kq = lax.dot_general(jnp.concatenate([kb.astype(BF16), q], axis=0), k,
                         (((1,), (1,)), ((), ())), preferred_element_type=F32)
    gam = jnp.exp(jnp.where(seen, gcol - grow, -jnp.inf))
    neg_l = jnp.where(before, -(kq[0:c] * gam), 0.0)
    qk = kq[c:2 * c] * gam
    y = _unit_tri_solve(neg_l, jnp.concatenate([v.astype(F32) * bcol, kb * eg], axis=1), ri, ci)
    value, kcum = y[:, 0:HEAD_DIM], y[:, HEAD_DIM:2 * HEAD_DIM]
    q_dec = q.astype(F32) * eg
    k_dec = kf * jnp.exp(g_end - gcol)
    vs = jnp.dot(jnp.concatenate([kcum.astype(BF16), q_dec.astype(BF16)], axis=0), s.astype(BF16),
                 preferred_element_type=F32)
    v_new = (value - vs[0:c]).astype(BF16)
    o = vs[c:2 * c] + jnp.dot(qk.astype(BF16), v_new, preferred_element_type=F32)
    s_new = s * jnp.exp(g_end) + lax.dot_general(k_dec.astype(BF16), v_new, (((0,), (0,)), ((), ())),
                                                 preferred_element_type=F32)
    return o, s_new


def _gdn_core_kernel(qf_ref, kf_ref, vf_ref, qb_ref, kb_ref, vb_ref, gbf_ref, gbb_ref, gtf_ref, gtb_ref,
                     s0_ref, of_ref, ob_ref, sout_ref, s_ref, *, n_heads):
    n = pl.program_id(2)
    heads = qf_ref.shape[0]
    h0 = pl.program_id(1) * heads

    @pl.when(n == 0)
    def _():
        s_ref[...] = s0_ref[...]

    lane = lax.broadcasted_iota(jnp.int32, (1, LANES), 1)
    streams = ((qf_ref, kf_ref, vf_ref, gbf_ref, gtf_ref, of_ref, False),
               (qb_ref, kb_ref, vb_ref, gbb_ref, gtb_ref, ob_ref, True))
    for d, (q_ref, k_ref, v_ref, gb_ref, gt_ref, o_ref, reverse) in enumerate(streams):
        gb = gb_ref[...]
        for hh in range(heads):
            col = d * n_heads + h0 + hh
            gcol = jnp.sum(jnp.where(lane == col, gb, 0.0), axis=1, keepdims=True)
            bcol = jnp.sum(jnp.where(lane == col + N_DIRS * n_heads, gb, 0.0), axis=1, keepdims=True)
            grow = gt_ref[pl.ds(col, 1), :]
            o, s_new = _delta_block(q_ref[hh], k_ref[hh], v_ref[hh], gcol, bcol, grow, s_ref[hh, d], reverse)
            o_ref[hh] = o.astype(o_ref.dtype)
            s_ref[hh, d] = s_new

    @pl.when(n == pl.num_programs(2) - 1)
    def _():
        sout_ref[...] = s_ref[...]


def _gdn_core(qkvg, gb, gbt, s0, n_seq, seq_len, n_heads, o_dtype):
    rows = n_seq * seq_len
    c = SCAN_BLOCK
    assert seq_len % c == 0
    nc = seq_len // c
    g = min(CORE_HEADS, n_heads)
    assert n_heads % g == 0
    ng = n_heads // g

    def fwd(b, n):
        return b * nc + n

    def bwd(b, n):
        return b * nc + nc - 1 - n

    def head_spec(part, blk):
        return pl.BlockSpec((g, c, HEAD_DIM), lambda b, hg, n: (part * ng + hg, blk(b, n), 0))

    state_spec = pl.BlockSpec((None, g, N_DIRS, HEAD_DIM, HEAD_DIM), lambda b, hg, n: (b, hg, 0, 0, 0))
    kern = functools.partial(_gdn_core_kernel, n_heads=n_heads)
    return pl.pallas_call(
        kern,
        grid=(n_seq, ng, nc),
        in_specs=[head_spec(0, fwd), head_spec(1, fwd), head_spec(2, fwd),
                  head_spec(0, bwd), head_spec(1, bwd), head_spec(2, bwd),
                  pl.BlockSpec((c, LANES), lambda b, hg, n: (fwd(b, n), 0)),
                  pl.BlockSpec((c, LANES), lambda b, hg, n: (bwd(b, n), 0)),
                  pl.BlockSpec((LANES, c), lambda b, hg, n: (0, fwd(b, n))),
                  pl.BlockSpec((LANES, c), lambda b, hg, n: (0, bwd(b, n))),
                  state_spec],
        out_specs=[pl.BlockSpec((g, c, HEAD_DIM), lambda b, hg, n: (hg, fwd(b, n), 0)),
                   pl.BlockSpec((g, c, HEAD_DIM), lambda b, hg, n: (hg, bwd(b, n), 0)),
                   state_spec],
        out_shape=[jax.ShapeDtypeStruct((n_heads, rows, HEAD_DIM), o_dtype),
                   jax.ShapeDtypeStruct((n_heads, rows, HEAD_DIM), o_dtype),
                   jax.ShapeDtypeStruct(s0.shape, F32)],
        scratch_shapes=[pltpu.VMEM((g, N_DIRS, HEAD_DIM, HEAD_DIM), F32)],
        compiler_params=_params("parallel", "parallel", "arbitrary"),
        name="gdn_core",
    )(qkvg, qkvg, qkvg, qkvg, qkvg, qkvg, gb, gb, gbt, gbt, s0)


def _gdn_combine_kernel(of_ref, ob_ref, gate_ref, g_ref, o_ref):
    for hh in range(of_ref.shape[0]):
        o = of_ref[hh].astype(F32) + ob_ref[hh].astype(F32)
        on = _rmsnorm(o, g_ref[...])
        o_ref[:, hh * HEAD_DIM:(hh + 1) * HEAD_DIM] = (on * _silu(gate_ref[hh].astype(F32))).astype(o_ref.dtype)


def _gdn_combine(o_f, o_b, qkvg, norm_g, tm):
    n_heads, rows, _ = o_f.shape
    g = min(CORE_HEADS, n_heads)
    ng = n_heads // g
    return pl.pallas_call(
        _gdn_combine_kernel,
        grid=(rows // tm, ng),
        in_specs=[pl.BlockSpec((g, tm, HEAD_DIM), lambda i, j: (j, i, 0)),
                  pl.BlockSpec((g, tm, HEAD_DIM), lambda i, j: (j, i, 0)),
                  pl.BlockSpec((g, tm, HEAD_DIM), lambda i, j: (3 * ng + j, i, 0)),
                  pl.BlockSpec((1, HEAD_DIM), lambda i, j: (0, 0))],
        out_specs=pl.BlockSpec((tm, g * HEAD_DIM), lambda i, j: (i, j)),
        out_shape=jax.ShapeDtypeStruct((rows, n_heads * HEAD_DIM), BF16),
        compiler_params=_params("parallel", "parallel"),
        name="gdn_combine",
    )(o_f, o_b, qkvg, norm_g.reshape(1, HEAD_DIM))


def _mm_out_kernel(*refs, gate_row, next_rows, has_scale, emit_y):
    it = iter(refs)
    lhs_ref, w_ref, y_ref, gate_ref = next(it), next(it), next(it), next(it)
    scale_ref = next(it) if has_scale else None
    gn_ref = next(it)
    modn_ref = next(it) if next_rows is not None else None
    ynew_ref = next(it) if emit_y else None
    hn_ref, row_ref = next(it), next(it)
    j = pl.program_id(1)
    nj, _, tn = row_ref.shape

    acc = jnp.dot(lhs_ref[...], w_ref[...], preferred_element_type=F32)
    if has_scale:
        acc = acc * scale_ref[...]
    y_new = y_ref[...] + gate_ref[gate_row:gate_row + 1, :] * acc
    if emit_y:
        ynew_ref[...] = y_new
    row_ref[j] = y_new

    @pl.when(j == nj - 1)
    def _():
        ss = jnp.zeros((row_ref.shape[1], 1), F32)
        for jj in range(nj):
            yj = row_ref[jj]
            ss = ss + jnp.sum(yj * yj, axis=-1, keepdims=True)
        inv = lax.rsqrt(ss / (nj * tn) + NORM_EPS)
        for jj in range(nj):
            cols = slice(jj * tn, (jj + 1) * tn)
            hj = row_ref[jj] * inv * gn_ref[:, cols]
            if next_rows is not None:
                sh, sc = next_rows
                hj = hj * (1.0 + modn_ref[sc:sc + 1, cols]) + modn_ref[sh:sh + 1, cols]
            hn_ref[:, cols] = hj.astype(hn_ref.dtype)


def _mm_out(lhs, w, y, modr, layer, gate_row, st, next_g, next_mod, out_dtype, col_scale=None, emit_y=True):
    rows, n = y.shape
    grouped = w.ndim == 3
    k = w.shape[-2]
    group_n = w.shape[-1]
    tn = _col_tile(group_n)
    nj = n // tn
    per_group = group_n // tn
    if grouped:
        lhs_spec = pl.BlockSpec((st.tm, k), lambda i, j: (i, j // per_group))
        w_spec = pl.BlockSpec((None, k, tn), lambda i, j: (j // per_group, 0, j % per_group))
    else:
        lhs_spec = pl.BlockSpec((st.tm, k), lambda i, j: (i, 0))
        w_spec = pl.BlockSpec((k, tn), lambda i, j: (0, j))
    in_specs = [lhs_spec, w_spec,
                pl.BlockSpec((st.tm, tn), lambda i, j: (i, j)),
                st.mod_spec(layer, n, tn)]
    args = [lhs, w, y, modr]
    if col_scale is not None:
        in_specs.append(pl.BlockSpec((1, tn), lambda i, j: (0, j)))
        args.append(col_scale.reshape(1, n))
    in_specs.append(pl.BlockSpec((1, n), lambda i, j: (0, 0)))
    args.append(next_g.reshape(1, n))
    next_rows = None
    if next_mod is not None:
        in_specs.append(st.mod_spec(next_mod[0], n))
        args.append(modr)
        next_rows = (next_mod[1], next_mod[2])
    out_specs, out_shape = [], []
    if emit_y:
        out_specs.append(pl.BlockSpec((st.tm, tn), lambda i, j: (i, j)))
        out_shape.append(jax.ShapeDtypeStruct((rows, n), F32))
    out_specs.append(pl.BlockSpec((st.tm, n), lambda i, j: (i, 0)))
    out_shape.append(jax.ShapeDtypeStruct((rows, n), out_dtype))
    kern = functools.partial(_mm_out_kernel, gate_row=gate_row, next_rows=next_rows,
                             has_scale=col_scale is not None, emit_y=emit_y)
    res = pl.pallas_call(
        kern,
        grid=(st.n_tiles, nj),
        in_specs=in_specs,
        out_specs=out_specs,
        out_shape=out_shape,
        scratch_shapes=[pltpu.VMEM((nj, st.tm, tn), F32)],
        compiler_params=_params("parallel", "arbitrary"),
        name="mm_out",
    )(*args)
    return res if emit_y else (None, res[0])


def kernel(x, c, ctx, c_ctx, mod_w, mod_b, norm1_g, norm2_g, ffn_w_in, ffn_conv, ffn_w_out, pool_w, pool_scale, sc_w_in, sc_conv, sc_w_out, gdn_w_in, gdn_conv, gdn_a_log, gdn_dt_bias, gdn_norm_g, gdn_w_out, final_g):
    b, t, d = x.shape
    t_ctx = ctx.shape[1]
    depth = mod_w.shape[0]
    n_mix = 3
    n_heads = d // HEAD_DIM
    assert 2 * N_DIRS * n_heads <= LANES
    gdn_layers = [i for i in range(depth) if i % n_mix == 2]
    last_ctx = gdn_layers[-1] if gdn_layers else -1

    lat = _Stream(b, t, min(ROW_TILE, t))
    cst = _Stream(b, t_ctx, min(ROW_TILE, t_ctx), ctx_row=b)
    modr = _mod_vectors(c, c_ctx, mod_w, mod_b)

    y = x.reshape(b * t, d)
    yc = ctx.reshape(b * t_ctx, d)

    def h_dtype(i):
        return F32 if i % n_mix == 0 else BF16

    h = _norm_mod(y, modr, 0, norm1_g[0], lat, h_dtype(0))
    hc = _norm_mod(yc, modr, 0, norm1_g[0], cst, h_dtype(0)) if last_ctx >= 0 else None

    out = None
    for i in range(depth):
        kind, jx = i % n_mix, i // n_mix
        use_ctx = i <= last_ctx
        ctx_stream = i < last_ctx
        last = i == depth - 1
        streams = [(lat, h, y, t, GRID_W)]
        if ctx_stream:
            streams.append((cst, hc, yc, t_ctx, None))

        if kind == 2:
            w_in = gdn_w_in[jx]
            w_main = w_in[:, :4 * d].astype(BF16)
            w_ab = jnp.pad(w_in[:, 4 * d:], ((0, 0), (0, LANES - 2 * N_DIRS * n_heads))).astype(BF16)
            conv = gdn_conv[jx]
            ab_par = jnp.pad(jnp.stack([gdn_a_log[jx].reshape(-1), gdn_dt_bias[jx].reshape(-1)]),
                             ((0, 0), (0, LANES - N_DIRS * n_heads)))
            qc, gbc, gbtc = _gdn_in(hc, w_main, w_ab, conv, ab_par, cst, n_heads)
            s0 = jnp.zeros((b, n_heads, N_DIRS, HEAD_DIM, HEAD_DIM), F32)
            ocf, ocb, s_ctx = _gdn_core(qc, gbc, gbtc, s0, b, t_ctx, n_heads, BF16)
            ql, gbl, gbtl = _gdn_in(h, w_main, w_ab, conv, ab_par, lat, n_heads)
            o_f, o_b, _ = _gdn_core(ql, gbl, gbtl, s_ctx, b, t, n_heads, F32)
            mixed = [(_gdn_combine(o_f, o_b, ql, gdn_norm_g[jx], lat.tm), gdn_w_out[jx].astype(BF16), None)]
            if ctx_stream:
                mixed.append((_gdn_combine(ocf, ocb, qc, gdn_norm_g[jx], cst.tm),
                              gdn_w_out[jx].astype(BF16), None))
        elif kind == 1:
            w_in, w_out = sc_w_in[jx].astype(BF16), sc_w_out[jx].astype(BF16)
            mixed = [(_sc_in(hs, w_in, sc_conv[jx], st), w_out, None) for st, hs, _, _, _ in streams]
        else:
            w_out = pool_w[jx].astype(BF16)
            mixed = [(_pool_delta(hs, b, sl, gw), w_out, pool_scale[jx]) for st, hs, _, sl, gw in streams]

        w_ffn_in, w_ffn_out = ffn_w_in[i].astype(BF16), ffn_w_out[i].astype(BF16)
        new = []
        for (st, _, ys, _, _), (m_lhs, w_out, scale) in zip(streams, mixed):
            y1, h2 = _mm_out(m_lhs, w_out, ys, modr, i, 2, st, norm2_g[i], (i, 3, 4), BF16, col_scale=scale)
            f = _ffn_in(h2, w_ffn_in, ffn_conv[i], st)
            if last:
                _, hn = _mm_out(f, w_ffn_out, y1, modr, i, 5, st, final_g, None, F32, emit_y=False)
                y2 = None
            else:
                y2, hn = _mm_out(f, w_ffn_out, y1, modr, i, 5, st, norm1_g[i + 1], (i + 1, 0, 1), h_dtype(i + 1))
            new.append((y2, hn))
        y, h = new[0]
        if ctx_stream:
            yc, hc = new[1]
        if last:
            out = h
    return out.reshape(b, t, d).astype(x.dtype)
```

```python
import functools
import math

import jax
import jax.numpy as jnp
from jax import lax
from jax.experimental import pallas as pl
from jax.experimental.pallas import tpu as pltpu

F32 = jnp.float32
BF16 = jnp.bfloat16

GRID_W = 64
POOL_WINDOWS = (2, 4, 8, 16)
HEAD_DIM = 128
N_DIRS = 2
NORM_EPS = 1e-6
SCAN_BLOCK = 128
HALO = 16
LANES = 128
VMEM_LIMIT_BYTES = 56 * 1024 * 1024
ROW_TILE = 512
COL_TILE = 512
POOL_COLS = 256
CORE_HEADS = 4


def _params(*sem):
    return pltpu.CompilerParams(dimension_semantics=sem, vmem_limit_bytes=VMEM_LIMIT_BYTES)


def _col_tile(n, cap=None):
    cap = COL_TILE if cap is None else cap
    for t in (cap, 512, 256, 128):
        if t <= cap and n % t == 0:
            return t
    raise ValueError(f"no column tile for {n}")


def _silu(x):
    return x * jax.nn.sigmoid(x)


def _rmsnorm(y, g):
    return y * lax.rsqrt(jnp.mean(y * y, axis=-1, keepdims=True) + NORM_EPS) * g


def _mod_kernel(c_ref, w_ref, b_ref, o_ref):
    s = _silu(c_ref[...]).astype(BF16)
    o_ref[...] = jnp.dot(s, w_ref[...].astype(BF16), preferred_element_type=F32) + b_ref[...]


def _mod_vectors(c, c_ctx, mod_w, mod_b):
    depth, d, d6 = mod_w.shape
    b = c.shape[0]
    r = -(-(b + 1) // 16) * 16
    cc = jnp.zeros((r, d), F32).at[:b].set(c).at[b].set(c_ctx)
    tn = _col_tile(d6)
    out = pl.pallas_call(
        _mod_kernel,
        grid=(depth, d6 // tn),
        in_specs=[pl.BlockSpec((r, d), lambda l, j: (0, 0)),
                  pl.BlockSpec((None, d, tn), lambda l, j: (l, 0, j)),
                  pl.BlockSpec((None, 1, tn), lambda l, j: (l, 0, j))],
        out_specs=pl.BlockSpec((None, r, tn), lambda l, j: (l, 0, j)),
        out_shape=jax.ShapeDtypeStruct((depth, r, d6), F32),
        compiler_params=_params("parallel", "parallel"),
        name="mod",
    )(cc, mod_w, mod_b.reshape(depth, 1, d6))
    return out.reshape(depth, r, 6, d)


class _Stream:
    def __init__(self, n_seq, seq_len, tm, ctx_row=None):
        assert seq_len % tm == 0 and tm % HALO == 0
        self.n_seq, self.seq_len, self.tm = n_seq, seq_len, tm
        self.tiles_per_seq = seq_len // tm
        self.rows = n_seq * seq_len
        self.n_tiles = self.rows // tm
        self.ctx_row = ctx_row

    def mod_row(self, i):
        return i // self.tiles_per_seq if self.ctx_row is None else self.ctx_row

    def mod_spec(self, layer, d, cols=None):
        if cols is None:
            return pl.BlockSpec((None, None, 6, d), lambda i, j: (layer, self.mod_row(i), 0, 0))
        return pl.BlockSpec((None, None, 6, cols), lambda i, j: (layer, self.mod_row(i), 0, j))


def _norm_kernel(y_ref, mod_ref, g_ref, o_ref):
    m = mod_ref[...]
    h = _rmsnorm(y_ref[...], g_ref[...]) * (1.0 + m[1:2]) + m[0:1]
    o_ref[...] = h.astype(o_ref.dtype)


def _norm_mod(y, modr, layer, g, st, out_dtype):
    rows, d = y.shape
    return pl.pallas_call(
        _norm_kernel,
        grid=(st.n_tiles, 1),
        in_specs=[pl.BlockSpec((st.tm, d), lambda i, j: (i, 0)),
                  st.mod_spec(layer, d),
                  pl.BlockSpec((1, d), lambda i, j: (0, 0))],
        out_specs=pl.BlockSpec((st.tm, d), lambda i, j: (i, 0)),
        out_shape=jax.ShapeDtypeStruct((rows, d), out_dtype),
        compiler_params=_params("parallel", "arbitrary"),
        name="norm_mod",
    )(y, modr, g.reshape(1, d))


def _window_sum(x, pos, n, unit, w):
    tot = x.shape[0]
    fwd, bwd = w - w // 2, w // 2
    assert fwd & (fwd - 1) == 0 and bwd & (bwd - 1) == 0 and bwd >= 1
    a, s = x, 1
    while s < fwd:
        a = a + jnp.where(pos + s < n, pltpu.roll(a, tot - s * unit, 0), 0.0)
        s *= 2
    b, s = jnp.where(pos >= 1, pltpu.roll(x, unit, 0), 0.0), 1
    while s < bwd:
        b = b + jnp.where(pos - s >= 0, pltpu.roll(b, s * unit, 0), 0.0)
        s *= 2
    return a + b


def _window_count(pos, n, w):
    return (jnp.minimum(pos + (w - w // 2), n) - jnp.maximum(pos - w // 2, 0)).astype(F32)


def _pool_kernel(h_ref, o_ref, *, tiles_per_group, grid_w, grid_rows):
    group = pl.program_id(1) // tiles_per_group
    t = h_ref.shape[0]
    tok = lax.broadcasted_iota(jnp.int32, (t, 1), 0)
    for gi, w in enumerate(POOL_WINDOWS):
        @pl.when(group == gi)
        def _(w=w):
            x = h_ref[...]
            if grid_rows is None:
                s = _window_sum(x, tok, t, 1, w)
                cnt = _window_count(tok, t, w)
            else:
                shift = grid_w.bit_length() - 1
                col, row = tok & (grid_w - 1), tok >> shift
                s = _window_sum(x, row, grid_rows, grid_w, w)
                s = _window_sum(s, col, grid_w, 1, w)
                cnt = _window_count(row, grid_rows, w) * _window_count(col, grid_w, w)
            o_ref[...] = (s / cnt - x).astype(o_ref.dtype)


def _pool_delta(h, n_seq, seq_len, grid_w):
    rows, d = h.shape
    group = d // len(POOL_WINDOWS)
    ct = min(POOL_COLS, group)
    assert group % ct == 0
    if grid_w is not None:
        assert grid_w & (grid_w - 1) == 0 and seq_len % grid_w == 0
    kern = functools.partial(_pool_kernel, tiles_per_group=group // ct, grid_w=grid_w,
                             grid_rows=None if grid_w is None else seq_len // grid_w)
    return pl.pallas_call(
        kern,
        grid=(n_seq, d // ct),
        in_specs=[pl.BlockSpec((seq_len, ct), lambda b, j: (b, j))],
        out_specs=pl.BlockSpec((seq_len, ct), lambda b, j: (b, j)),
        out_shape=jax.ShapeDtypeStruct((rows, d), BF16),
        compiler_params=_params("parallel", "parallel"),
        name="pool",
    )(h)


def _halo_specs(st, d):
    per = st.tm // HALO
    last = st.rows // HALO - 1
    return [pl.BlockSpec((st.tm, d), lambda i, j: (i, 0)),
            pl.BlockSpec((HALO, d), lambda i, j: (jnp.maximum(i * per - 1, 0), 0)),
            pl.BlockSpec((HALO, d), lambda i, j: (jnp.minimum((i + 1) * per, last), 0))]


def _fill_lhs(lhs_ref, hm_ref, hp_ref, hn_ref, tiles_per_seq):
    tm = hm_ref.shape[0]
    i = pl.program_id(0) % tiles_per_seq
    lhs_ref[0:tm, :] = hm_ref[...]
    r = lax.broadcasted_iota(jnp.int32, hp_ref.shape, 0)
    nxt = jnp.where(i == tiles_per_seq - 1, jnp.zeros_like(hn_ref[...]), hn_ref[...])
    prv = jnp.where(i == 0, jnp.zeros_like(hp_ref[...]), hp_ref[...])
    lhs_ref[tm:tm + HALO, :] = jnp.where(r < HALO // 2, nxt, prv)


def _conv3(a_ext, cw, tm):
    n = a_ext.shape[0]
    dn = pltpu.roll(a_ext, 1, 0)[0:tm]
    up = pltpu.roll(a_ext, n - 1, 0)[0:tm]
    return cw[0:1] * dn + cw[1:2] * a_ext[0:tm] + cw[2:3] * up


def _ffn_in_kernel(hm_ref, hp_ref, hn_ref, wa_ref, wu_ref, cw_ref, o_ref, lhs_ref, *, tiles_per_seq):
    tm = hm_ref.shape[0]

    @pl.when(pl.program_id(1) == 0)
    def _():
        _fill_lhs(lhs_ref, hm_ref, hp_ref, hn_ref, tiles_per_seq)

    a = jnp.dot(lhs_ref[...], wa_ref[...], preferred_element_type=F32)
    u = jnp.dot(lhs_ref[0:tm, :], wu_ref[...], preferred_element_type=F32)
    o_ref[...] = (_silu(_conv3(a, cw_ref[...], tm)) * u).astype(o_ref.dtype)


def _ffn_in(h, w_in, conv, st):
    rows, d = h.shape
    f = conv.shape[1]
    tn = _col_tile(f)
    nj = f // tn
    kern = functools.partial(_ffn_in_kernel, tiles_per_seq=st.tiles_per_seq)
    return pl.pallas_call(
        kern,
        grid=(st.n_tiles, nj),
        in_specs=_halo_specs(st, d) + [
            pl.BlockSpec((d, tn), lambda i, j: (0, j)),
            pl.BlockSpec((d, tn), lambda i, j: (0, j + nj)),
            pl.BlockSpec((3, tn), lambda i, j: (0, j))],
        out_specs=pl.BlockSpec((st.tm, tn), lambda i, j: (i, j)),
        out_shape=jax.ShapeDtypeStruct((rows, f), BF16),
        scratch_shapes=[pltpu.VMEM((st.tm + HALO, d), BF16)],
        compiler_params=_params("parallel", "arbitrary"),
        name="ffn_in",
    )(h, h, h, w_in, w_in, conv)


def _sc_in_kernel(hm_ref, hp_ref, hn_ref, wb_ref, wc_ref, wv_ref, cw_ref, o_ref, lhs_ref, *, tiles_per_seq):
    tm = hm_ref.shape[0]

    @pl.when(pl.program_id(1) == 0)
    def _():
        _fill_lhs(lhs_ref, hm_ref, hp_ref, hn_ref, tiles_per_seq)

    lhs = lhs_ref[...]
    cg = jnp.dot(lhs, wc_ref[...], preferred_element_type=F32)
    v = jnp.dot(lhs, wv_ref[...], preferred_element_type=F32)
    bg = jnp.dot(lhs_ref[0:tm, :], wb_ref[...], preferred_element_type=F32)
    o_ref[...] = (bg * _conv3(cg * v, cw_ref[...], tm)).astype(o_ref.dtype)


def _sc_in(h, w_in, conv, st):
    rows, d = h.shape
    n = conv.shape[1]
    tn = _col_tile(n)
    nj = n // tn
    kern = functools.partial(_sc_in_kernel, tiles_per_seq=st.tiles_per_seq)
    return pl.pallas_call(
        kern,
        grid=(st.n_tiles, nj),
        in_specs=_halo_specs(st, d) + [
            pl.BlockSpec((d, tn), lambda i, j: (0, j)),
            pl.BlockSpec((d, tn), lambda i, j: (0, j + nj)),
            pl.BlockSpec((d, tn), lambda i, j: (0, j + 2 * nj)),
            pl.BlockSpec((3, tn), lambda i, j: (0, j))],
        out_specs=pl.BlockSpec((st.tm, tn), lambda i, j: (i, j)),
        out_shape=jax.ShapeDtypeStruct((rows, n), BF16),
        scratch_shapes=[pltpu.VMEM((st.tm + HALO, d), BF16)],
        compiler_params=_params("parallel", "arbitrary"),
        name="sc_in",
    )(h, h, h, w_in, w_in, w_in, conv)


def _block_scan(x, pos, block, reverse):
    n = x.shape[0]
    s = 1
    while s < block:
        if reverse:
            x = x + jnp.where(pos + s < block, pltpu.roll(x, n - s, 0), 0.0)
        else:
            x = x + jnp.where(pos >= s, pltpu.roll(x, s, 0), 0.0)
        s *= 2
    return x


def _gdn_in_kernel(hm_ref, hp_ref, hn_ref, w_ref, wab_ref, cw_ref, ab_ref, o_ref, gb_ref, gbt_ref, lhs_ref,
                   *, tiles_per_seq, n_qk_tiles, n_conv_tiles, n_heads):
    tm = hm_ref.shape[0]
    j = pl.program_id(1)
    tn = w_ref.shape[1]

    @pl.when(j == 0)
    def _():
        _fill_lhs(lhs_ref, hm_ref, hp_ref, hn_ref, tiles_per_seq)
        ab = jnp.dot(lhs_ref[0:tm, :], wab_ref[...], preferred_element_type=F32)
        p = ab_ref[...]
        x = ab + p[1:2]
        log_alpha = -jnp.exp(p[0:1]) * (jnp.maximum(x, 0.0) + jnp.log1p(jnp.exp(-jnp.abs(x))))
        beta = jax.nn.sigmoid(ab)
        pos = lax.broadcasted_iota(jnp.int32, (tm, 1), 0) & (SCAN_BLOCK - 1)
        lane = lax.broadcasted_iota(jnp.int32, (1, LANES), 1)
        g = jnp.where(lane < n_heads, _block_scan(log_alpha, pos, SCAN_BLOCK, False),
                      _block_scan(log_alpha, pos, SCAN_BLOCK, True))
        gb = jnp.where(lane < N_DIRS * n_heads, g, beta)
        gb_ref[...] = gb
        gbt_ref[...] = gb.T

    a = jnp.dot(lhs_ref[...], w_ref[...], preferred_element_type=F32)
    x = jnp.where(j < n_conv_tiles, _silu(_conv3(a, cw_ref[...], tm)), a[0:tm])
    scale = jnp.where(j < n_qk_tiles // 2, HEAD_DIM ** -0.5, 1.0).astype(F32)
    for hh in range(tn // HEAD_DIM):
        xh = x[:, hh * HEAD_DIM:(hh + 1) * HEAD_DIM]
        inv = lax.rsqrt(jnp.sum(xh * xh, axis=-1, keepdims=True) + NORM_EPS) * scale
        o_ref[hh] = (xh * jnp.where(j < n_qk_tiles, inv, 1.0)).astype(o_ref.dtype)


def _gdn_in(h, w_main, w_ab, conv, ab_par, st, n_heads):
    rows, d = h.shape
    n = w_main.shape[1]
    tn = _col_tile(d)
    nj = n // tn
    n_conv = conv.shape[1] // tn
    kern = functools.partial(_gdn_in_kernel, tiles_per_seq=st.tiles_per_seq, n_qk_tiles=2 * d // tn,
                             n_conv_tiles=n_conv, n_heads=n_heads)
    hpt = tn // HEAD_DIM
    return pl.pallas_call(
        kern,
        grid=(st.n_tiles, nj),
        in_specs=_halo_specs(st, d) + [
            pl.BlockSpec((d, tn), lambda i, j: (0, j)),
            pl.BlockSpec((d, LANES), lambda i, j: (0, 0)),
            pl.BlockSpec((3, tn), lambda i, j: (0, jnp.minimum(j, n_conv - 1))),
            pl.BlockSpec((2, LANES), lambda i, j: (0, 0))],
        out_specs=[pl.BlockSpec((hpt, st.tm, HEAD_DIM), lambda i, j: (j, i, 0)),
                   pl.BlockSpec((st.tm, LANES), lambda i, j: (i, 0)),
                   pl.BlockSpec((LANES, st.tm), lambda i, j: (0, i))],
        out_shape=[jax.ShapeDtypeStruct((n // HEAD_DIM, rows, HEAD_DIM), BF16),
                   jax.ShapeDtypeStruct((rows, LANES), F32),
                   jax.ShapeDtypeStruct((LANES, rows), F32)],
        scratch_shapes=[pltpu.VMEM((st.tm + HALO, d), BF16)],
        compiler_params=_params("parallel", "arbitrary"),
        name="gdn_in",
    )(h, h, h, w_main, w_ab, conv, ab_par)


def _each(f, *lists):
    return [f(*xs) for xs in zip(*lists)]


def _bdot(a, b):
    return _each(lambda x, y: jnp.dot(x.astype(BF16), y.astype(BF16), preferred_element_type=F32), a, b)


def _unit_tri_solve(neg_l, rhs, ri, ci):
    c = neg_l[0].shape[0]
    base = 8

    def same_block(width):
        sh = width.bit_length() - 1
        return (ri >> sh) == (ci >> sh)

    diag = same_block(base)
    n0 = [jnp.where(diag, m, 0.0) for m in neg_l]
    p2 = _bdot(n0, n0)
    t = _bdot([jnp.concatenate([a, b], axis=0) for a, b in zip(n0, p2)], p2)
    x = _each(lambda a, b, tt: a + b + tt[0:c], n0, p2, t)
    p4 = [tt[c:2 * c] for tt in t]
    x = _each(lambda xx, p, u: xx + p + u, x, p4, _bdot(x, p4))
    size = base
    while size < c:
        pair = jnp.logical_and(same_block(2 * size), jnp.logical_not(same_block(size)))
        cm = [jnp.where(pair, m, 0.0) for m in neg_l]
        a = _each(lambda m, u: m + u, cm, _bdot(x, cm))
        x = _each(lambda xx, aa, u: xx + aa + u, x, a, _bdot(a, x))
        size *= 2
    return _each(lambda r, u: r + u, rhs, _bdot(x, rhs))


def _delta_blocks(q, k, v, gcol, bcol, grow, s, reverse):
    c = q[0].shape[0]
    ri = lax.broadcasted_iota(jnp.int32, (c, c), 0)
    ci = lax.broadcasted_iota(jnp.int32, (c, c), 1)
    seen = [(ri <= ci) if r else (ri >= ci) for r in reverse]
    before = [(ri < ci) if r else (ri > ci) for r in reverse]
    g_end = [g[0:1] if r else g[c - 1:c] for g, r in zip(gcol, reverse)]
    kf = [x.astype(F32) for x in k]
    kb = _each(lambda x, b: x * b, kf, bcol)
    eg = [jnp.exp(g) for g in gcol]
    kq = _each(lambda x, y, z: lax.dot_general(jnp.concatenate([x.astype(BF16), y], axis=0), z,
                                               (((1,), (1,)), ((), ())), preferred_element_type=F32), kb, q, k)
    gam = _each(lambda m, gc, gr: jnp.exp(jnp.where(m, gc - gr, -jnp.inf)), seen, gcol, grow)
    neg_l = _each(lambda m, x, gm: jnp.where(m, -(x[0:c] * gm), 0.0), before, kq, gam)
    qk = _each(lambda x, gm: x[c:2 * c] * gm, kq, gam)
    rhs = _each(lambda x, b, y, e: jnp.concatenate([x.astype(F32) * b, y * e], axis=1), v, bcol, kb, eg)
    y = _unit_tri_solve(neg_l, rhs, ri, ci)
    q_dec = _each(lambda x, e: x.astype(F32) * e, q, eg)
    k_dec = _each(lambda x, ge, g: x * jnp.exp(ge - g), kf, g_end, gcol)
    vs = _bdot(_each(lambda yy, qd: jnp.concatenate([yy[:, HEAD_DIM:2 * HEAD_DIM], qd], axis=0), y, q_dec), s)
    v_new = _each(lambda yy, u: (yy[:, 0:HEAD_DIM] - u[0:c]).astype(BF16), y, vs)
    o = _each(lambda u, w: u[c:2 * c] + w, vs, _bdot(qk, v_new))
    kv = _each(lambda x, w: lax.dot_general(x.astype(BF16), w, (((0,), (0,)), ((), ())),
                                            preferred_element_type=F32), k_dec, v_new)
    s_new = _each(lambda ss, ge, u: ss * jnp.exp(ge) + u, s, g_end, kv)
    return o, s_new


def _gdn_core_kernel(qf_ref, kf_ref, vf_ref, qb_ref, kb_ref, vb_ref, gbf_ref, gbb_ref, gtf_ref, gtb_ref,
                     s0_ref, of_ref, ob_ref, sout_ref, s_ref, *, n_heads):
    n = pl.program_id(2)
    heads = qf_ref.shape[0]
    h0 = pl.program_id(1) * heads

    @pl.when(n == 0)
    def _():
        s_ref[...] = s0_ref[...]

    lane = lax.broadcasted_iota(jnp.int32, (1, LANES), 1)
    dirs = ((qf_ref, kf_ref, vf_ref, gbf_ref, gtf_ref, of_ref, False),
            (qb_ref, kb_ref, vb_ref, gbb_ref, gtb_ref, ob_ref, True))
    q, k, v, gcol, bcol, grow, s, reverse, dest = [], [], [], [], [], [], [], [], []
    for d, (q_ref, k_ref, v_ref, gb_ref, gt_ref, o_ref, rev) in enumerate(dirs):
        gb = gb_ref[...]
        for hh in range(heads):
            col = d * n_heads + h0 + hh
            gcol.append(jnp.sum(jnp.where(lane == col, gb, 0.0), axis=1, keepdims=True))
            bcol.append(jnp.sum(jnp.where(lane == col + N_DIRS * n_heads, gb, 0.0), axis=1, keepdims=True))
            grow.append(gt_ref[pl.ds(col, 1), :])
            q.append(q_ref[hh])
            k.append(k_ref[hh])
            v.append(v_ref[hh])
            s.append(s_ref[hh, d])
            reverse.append(rev)
            dest.append((o_ref, hh, d))
    o, s_new = _delta_blocks(q, k, v, gcol, bcol, grow, s, reverse)
    for (o_ref, hh, d), oo, ss in zip(dest, o, s_new):
        o_ref[hh] = oo.astype(o_ref.dtype)
        s_ref[hh, d] = ss

    @pl.when(n == pl.num_programs(2) - 1)
    def _():
        sout_ref[...] = s_ref[...]


def _gdn_core(qkvg, gb, gbt, s0, n_seq, seq_len, n_heads, o_dtype):
    rows = n_seq * seq_len
    c = SCAN_BLOCK
    assert seq_len % c == 0
    nc = seq_len // c
    g = min(CORE_HEADS, n_heads)
    assert n_heads % g == 0
    ng = n_heads // g

    def fwd(b, n):
        return b * nc + n

    def bwd(b, n):
        return b * nc + nc - 1 - n

    def head_spec(part, blk):
        return pl.BlockSpec((g, c, HEAD_DIM), lambda b, hg, n: (part * ng + hg, blk(b, n), 0))

    state_spec = pl.BlockSpec((None, g, N_DIRS, HEAD_DIM, HEAD_DIM), lambda b, hg, n: (b, hg, 0, 0, 0))
    kern = functools.partial(_gdn_core_kernel, n_heads=n_heads)
    return pl.pallas_call(
        kern,
        grid=(n_seq, ng, nc),
        in_specs=[head_spec(0, fwd), head_spec(1, fwd), head_spec(2, fwd),
                  head_spec(0, bwd), head_spec(1, bwd), head_spec(2, bwd),
                  pl.BlockSpec((c, LANES), lambda b, hg, n: (fwd(b, n), 0)),
                  pl.BlockSpec((c, LANES), lambda b, hg, n: (bwd(b, n), 0)),
                  pl.BlockSpec((LANES, c), lambda b, hg, n: (0, fwd(b, n))),
                  pl.BlockSpec((LANES, c), lambda b, hg, n: (0, bwd(b, n))),
                  state_spec],
        out_specs=[pl.BlockSpec((g, c, HEAD_DIM), lambda b, hg, n: (hg, fwd(b, n), 0)),
                   pl.BlockSpec((g, c, HEAD_DIM), lambda b, hg, n: (hg, bwd(b, n), 0)),
                   state_spec],
        out_shape=[jax.ShapeDtypeStruct((n_heads, rows, HEAD_DIM), o_dtype),
                   jax.ShapeDtypeStruct((n_heads, rows, HEAD_DIM), o_dtype),
                   jax.ShapeDtypeStruct(s0.shape, F32)],
        scratch_shapes=[pltpu.VMEM((g, N_DIRS, HEAD_DIM, HEAD_DIM), F32)],
        compiler_params=_params("parallel", "parallel", "arbitrary"),
        name="gdn_core",
    )(qkvg, qkvg, qkvg, qkvg, qkvg, qkvg, gb, gb, gbt, gbt, s0)


def _gdn_combine_kernel(of_ref, ob_ref, gate_ref, g_ref, o_ref):
    for hh in range(of_ref.shape[0]):
        o = of_ref[hh].astype(F32) + ob_ref[hh].astype(F32)
        on = _rmsnorm(o, g_ref[...])
        o_ref[:, hh * HEAD_DIM:(hh + 1) * HEAD_DIM] = (on * _silu(gate_ref[hh].astype(F32))).astype(o_ref.dtype)


def _gdn_combine(o_f, o_b, qkvg, norm_g, tm):
    n_heads, rows, _ = o_f.shape
    g = min(CORE_HEADS, n_heads)
    ng = n_heads // g
    return pl.pallas_call(
        _gdn_combine_kernel,
        grid=(rows // tm, ng),
        in_specs=[pl.BlockSpec((g, tm, HEAD_DIM), lambda i, j: (j, i, 0)),
                  pl.BlockSpec((g, tm, HEAD_DIM), lambda i, j: (j, i, 0)),
                  pl.BlockSpec((g, tm, HEAD_DIM), lambda i, j: (3 * ng + j, i, 0)),
                  pl.BlockSpec((1, HEAD_DIM), lambda i, j: (0, 0))],
        out_specs=pl.BlockSpec((tm, g * HEAD_DIM), lambda i, j: (i, j)),
        out_shape=jax.ShapeDtypeStruct((rows, n_heads * HEAD_DIM), BF16),
        compiler_params=_params("parallel", "parallel"),
        name="gdn_combine",
    )(o_f, o_b, qkvg, norm_g.reshape(1, HEAD_DIM))


def _mm_out_kernel(*refs, gate_row, next_rows, has_scale, emit_y):
    it = iter(refs)
    lhs_ref, w_ref, y_ref, gate_ref = next(it), next(it), next(it), next(it)
    scale_ref = next(it) if has_scale else None
    gn_ref = next(it)
    modn_ref = next(it) if next_rows is not None else None
    ynew_ref = next(it) if emit_y else None
    hn_ref, row_ref = next(it), next(it)
    j = pl.program_id(1)
    nj, _, tn = row_ref.shape

    acc = jnp.dot(lhs_ref[...], w_ref[...], preferred_element_type=F32)
    if has_scale:
        acc = acc * scale_ref[...]
    y_new = y_ref[...] + gate_ref[gate_row:gate_row + 1, :] * acc
    if emit_y:
        ynew_ref[...] = y_new
    row_ref[j] = y_new

    @pl.when(j == nj - 1)
    def _():
        ss = jnp.zeros((row_ref.shape[1], 1), F32)
        for jj in range(nj):
            yj = row_ref[jj]
            ss = ss + jnp.sum(yj * yj, axis=-1, keepdims=True)
        inv = lax.rsqrt(ss / (nj * tn) + NORM_EPS)
        for jj in range(nj):
            cols = slice(jj * tn, (jj + 1) * tn)
            hj = row_ref[jj] * inv * gn_ref[:, cols]
            if next_rows is not None:
                sh, sc = next_rows
                hj = hj * (1.0 + modn_ref[sc:sc + 1, cols]) + modn_ref[sh:sh + 1, cols]
            hn_ref[:, cols] = hj.astype(hn_ref.dtype)


def _mm_out(lhs, w, y, modr, layer, gate_row, st, next_g, next_mod, out_dtype, col_scale=None, emit_y=True):
    rows, n = y.shape
    grouped = w.ndim == 3
    k = w.shape[-2]
    group_n = w.shape[-1]
    tn = _col_tile(group_n)
    nj = n // tn
    per_group = group_n // tn
    if grouped:
        lhs_spec = pl.BlockSpec((st.tm, k), lambda i, j: (i, j // per_group))
        w_spec = pl.BlockSpec((None, k, tn), lambda i, j: (j // per_group, 0, j % per_group))
    else:
        lhs_spec = pl.BlockSpec((st.tm, k), lambda i, j: (i, 0))
        w_spec = pl.BlockSpec((k, tn), lambda i, j: (0, j))
    in_specs = [lhs_spec, w_spec,
                pl.BlockSpec((st.tm, tn), lambda i, j: (i, j)),
                st.mod_spec(layer, n, tn)]
    args = [lhs, w, y, modr]
    if col_scale is not None:
        in_specs.append(pl.BlockSpec((1, tn), lambda i, j: (0, j)))
        args.append(col_scale.reshape(1, n))
    in_specs.append(pl.BlockSpec((1, n), lambda i, j: (0, 0)))
    args.append(next_g.reshape(1, n))
    next_rows = None
    if next_mod is not None:
        in_specs.append(st.mod_spec(next_mod[0], n))
        args.append(modr)
        next_rows = (next_mod[1], next_mod[2])
    out_specs, out_shape = [], []
    if emit_y:
        out_specs.append(pl.BlockSpec((st.tm, tn), lambda i, j: (i, j)))
        out_shape.append(jax.ShapeDtypeStruct((rows, n), F32))
    out_specs.append(pl.BlockSpec((st.tm, n), lambda i, j: (i, 0)))
    out_shape.append(jax.ShapeDtypeStruct((rows, n), out_dtype))
    kern = functools.partial(_mm_out_kernel, gate_row=gate_row, next_rows=next_rows,
                             has_scale=col_scale is not None, emit_y=emit_y)
    res = pl.pallas_call(
        kern,
        grid=(st.n_tiles, nj),
        in_specs=in_specs,
        out_specs=out_specs,
        out_shape=out_shape,
        scratch_shapes=[pltpu.VMEM((nj, st.tm, tn), F32)],
        compiler_params=_params("parallel", "arbitrary"),
        name="mm_out",
    )(*args)
    return res if emit_y else (None, res[0])


def kernel(x, c, ctx, c_ctx, mod_w, mod_b, norm1_g, norm2_g, ffn_w_in, ffn_conv, ffn_w_out, pool_w, pool_scale, sc_w_in, sc_conv, sc_w_out, gdn_w_in, gdn_conv, gdn_a_log, gdn_dt_bias, gdn_norm_g, gdn_w_out, final_g):
    b, t, d = x.shape
    t_ctx = ctx.shape[1]
    depth = mod_w.shape[0]
    n_mix = 3
    n_heads = d // HEAD_DIM
    assert 2 * N_DIRS * n_heads <= LANES
    gdn_layers = [i for i in range(depth) if i % n_mix == 2]
    last_ctx = gdn_layers[-1] if gdn_layers else -1

    lat = _Stream(b, t, min(ROW_TILE, t))
    cst = _Stream(b, t_ctx, min(ROW_TILE, t_ctx), ctx_row=b)
    modr = _mod_vectors(c, c_ctx, mod_w, mod_b)

    y = x.reshape(b * t, d)
    yc = ctx.reshape(b * t_ctx, d)

    def h_dtype(i):
        return F32 if i % n_mix == 0 else BF16

    h = _norm_mod(y, modr, 0, norm1_g[0], lat, h_dtype(0))
    hc = _norm_mod(yc, modr, 0, norm1_g[0], cst, h_dtype(0)) if last_ctx >= 0 else None

    out = None
    for i in range(depth):
        kind, jx = i % n_mix, i // n_mix
        use_ctx = i <= last_ctx
        ctx_stream = i < last_ctx
        last = i == depth - 1
        streams = [(lat, h, y, t, GRID_W)]
        if ctx_stream:
            streams.append((cst, hc, yc, t_ctx, None))

        if kind == 2:
            w_in = gdn_w_in[jx]
            w_main = w_in[:, :4 * d].astype(BF16)
            w_ab = jnp.pad(w_in[:, 4 * d:], ((0, 0), (0, LANES - 2 * N_DIRS * n_heads))).astype(BF16)
            conv = gdn_conv[jx]
            ab_par = jnp.pad(jnp.stack([gdn_a_log[jx].reshape(-1), gdn_dt_bias[jx].reshape(-1)]),
                             ((0, 0), (0, LANES - N_DIRS * n_heads)))
            qc, gbc, gbtc = _gdn_in(hc, w_main, w_ab, conv, ab_par, cst, n_heads)
            s0 = jnp.zeros((b, n_heads, N_DIRS, HEAD_DIM, HEAD_DIM), F32)
            ocf, ocb, s_ctx = _gdn_core(qc, gbc, gbtc, s0, b, t_ctx, n_heads, BF16)
            ql, gbl, gbtl = _gdn_in(h, w_main, w_ab, conv, ab_par, lat, n_heads)
            o_f, o_b, _ = _gdn_core(ql, gbl, gbtl, s_ctx, b, t, n_heads, F32)
            mixed = [(_gdn_combine(o_f, o_b, ql, gdn_norm_g[jx], lat.tm), gdn_w_out[jx].astype(BF16), None)]
            if ctx_stream:
                mixed.append((_gdn_combine(ocf, ocb, qc, gdn_norm_g[jx], cst.tm),
                              gdn_w_out[jx].astype(BF16), None))
        elif kind == 1:
            w_in, w_out = sc_w_in[jx].astype(BF16), sc_w_out[jx].astype(BF16)
            mixed = [(_sc_in(hs, w_in, sc_conv[jx], st), w_out, None) for st, hs, _, _, _ in streams]
        else:
            w_out = pool_w[jx].astype(BF16)
            mixed = [(_pool_delta(hs, b, sl, gw), w_out, pool_scale[jx]) for st, hs, _, sl, gw in streams]

        w_ffn_in, w_ffn_out = ffn_w_in[i].astype(BF16), ffn_w_out[i].astype(BF16)
        new = []
        for (st, _, ys, _, _), (m_lhs, w_out, scale) in zip(streams, mixed):
            y1, h2 = _mm_out(m_lhs, w_out, ys, modr, i, 2, st, norm2_g[i], (i, 3, 4), BF16, col_scale=scale)
            f = _ffn_in(h2, w_ffn_in, ffn_conv[i], st)
            if last:
                _, hn = _mm_out(f, w_ffn_out, y1, modr, i, 5, st, final_g, None, F32, emit_y=False)
                y2 = None
            else:
                y2, hn = _mm_out(f, w_ffn_out, y1, modr, i, 5, st, norm1_g[i + 1], (i + 1, 0, 1), h_dtype(i + 1))
            new.append((y2, hn))
        y, h = new[0]
        if ctx_stream:
            yc, hc = new[1]
        if last:
            out = h
    return out.reshape(b, t, d).astype(x.dtype)
```

```python
import functools
import math

import jax
import jax.numpy as jnp
from jax import lax
from jax.experimental import pallas as pl
from jax.experimental.pallas import tpu as pltpu

F32 = jnp.float32
BF16 = jnp.bfloat16

GRID_W = 64
POOL_WINDOWS = (2, 4, 8, 16)
HEAD_DIM = 128
N_DIRS = 2
NORM_EPS = 1e-6
SCAN_BLOCK = 128
HALO = 16
LANES = 128
VMEM_LIMIT_BYTES = 56 * 1024 * 1024
ROW_TILE = 512
IN_ROW_TILE = 1024
COL_TILE = 512
GDN_COL_TILE = 512
GROUP_COL_TILE = 1024
POOL_COLS = 256
CORE_HEADS = 16


def _params(*sem):
    return pltpu.CompilerParams(dimension_semantics=sem, vmem_limit_bytes=VMEM_LIMIT_BYTES)


def _col_tile(n, cap=None):
    cap = COL_TILE if cap is None else cap
    for t in (cap, 512, 256, 128):
        if t <= cap and n % t == 0:
            return t
    raise ValueError(f"no column tile for {n}")


def _silu(x):
    return x * jax.nn.sigmoid(x)


def _rmsnorm(y, g):
    return y * lax.rsqrt(jnp.mean(y * y, axis=-1, keepdims=True) + NORM_EPS) * g


def _mod_kernel(c_ref, w_ref, b_ref, o_ref):
    s = _silu(c_ref[...]).astype(BF16)
    o_ref[...] = jnp.dot(s, w_ref[...].astype(BF16), preferred_element_type=F32) + b_ref[...]


def _mod_vectors(c, c_ctx, mod_w, mod_b):
    depth, d, d6 = mod_w.shape
    b = c.shape[0]
    r = -(-(b + 1) // 16) * 16
    cc = jnp.zeros((r, d), F32).at[:b].set(c).at[b].set(c_ctx)
    tn = _col_tile(d6)
    out = pl.pallas_call(
        _mod_kernel,
        grid=(depth, d6 // tn),
        in_specs=[pl.BlockSpec((r, d), lambda l, j: (0, 0)),
                  pl.BlockSpec((None, d, tn), lambda l, j: (l, 0, j)),
                  pl.BlockSpec((None, 1, tn), lambda l, j: (l, 0, j))],
        out_specs=pl.BlockSpec((None, r, tn), lambda l, j: (l, 0, j)),
        out_shape=jax.ShapeDtypeStruct((depth, r, d6), F32),
        compiler_params=_params("parallel", "parallel"),
        name="mod",
    )(cc, mod_w, mod_b.reshape(depth, 1, d6))
    return out.reshape(depth, r, 6, d)


class _Stream:
    def __init__(self, n_seq, seq_len, tm, ctx_row=None):
        assert seq_len % tm == 0 and tm % HALO == 0
        self.n_seq, self.seq_len, self.tm = n_seq, seq_len, tm
        self.tiles_per_seq = seq_len // tm
        self.rows = n_seq * seq_len
        self.n_tiles = self.rows // tm
        self.ctx_row = ctx_row

    def mod_row(self, i):
        return i // self.tiles_per_seq if self.ctx_row is None else self.ctx_row

    def mod_spec(self, layer, d, cols=None):
        if cols is None:
            return pl.BlockSpec((None, None, 6, d), lambda i, j: (layer, self.mod_row(i), 0, 0))
        return pl.BlockSpec((None, None, 6, cols), lambda i, j: (layer, self.mod_row(i), 0, j))


def _norm_kernel(y_ref, mod_ref, g_ref, o_ref):
    m = mod_ref[...]
    h = _rmsnorm(y_ref[...], g_ref[...]) * (1.0 + m[1:2]) + m[0:1]
    o_ref[...] = h.astype(o_ref.dtype)


def _norm_mod(y, modr, layer, g, st, out_dtype):
    rows, d = y.shape
    return pl.pallas_call(
        _norm_kernel,
        grid=(st.n_tiles, 1),
        in_specs=[pl.BlockSpec((st.tm, d), lambda i, j: (i, 0)),
                  st.mod_spec(layer, d),
                  pl.BlockSpec((1, d), lambda i, j: (0, 0))],
        out_specs=pl.BlockSpec((st.tm, d), lambda i, j: (i, 0)),
        out_shape=jax.ShapeDtypeStruct((rows, d), out_dtype),
        compiler_params=_params("parallel", "arbitrary"),
        name="norm_mod",
    )(y, modr, g.reshape(1, d))


def _window_sum(x, pos, n, unit, w):
    tot = x.shape[0]
    fwd, bwd = w - w // 2, w // 2
    assert fwd & (fwd - 1) == 0 and bwd & (bwd - 1) == 0 and bwd >= 1
    a, s = x, 1
    while s < fwd:
        a = a + jnp.where(pos + s < n, pltpu.roll(a, tot - s * unit, 0), 0.0)
        s *= 2
    b, s = jnp.where(pos >= 1, pltpu.roll(x, unit, 0), 0.0), 1
    while s < bwd:
        b = b + jnp.where(pos - s >= 0, pltpu.roll(b, s * unit, 0), 0.0)
        s *= 2
    return a + b


def _window_count(pos, n, w):
    return (jnp.minimum(pos + (w - w // 2), n) - jnp.maximum(pos - w // 2, 0)).astype(F32)


def _pool_kernel(h_ref, o_ref, *, tiles_per_group, grid_w, grid_rows):
    group = pl.program_id(1) // tiles_per_group
    t = h_ref.shape[0]
    tok = lax.broadcasted_iota(jnp.int32, (t, 1), 0)
    for gi, w in enumerate(POOL_WINDOWS):
        @pl.when(group == gi)
        def _(w=w):
            x = h_ref[...]
            if grid_rows is None:
                s = _window_sum(x, tok, t, 1, w)
                cnt = _window_count(tok, t, w)
            else:
                shift = grid_w.bit_length() - 1
                col, row = tok & (grid_w - 1), tok >> shift
                s = _window_sum(x, row, grid_rows, grid_w, w)
                s = _window_sum(s, col, grid_w, 1, w)
                cnt = _window_count(row, grid_rows, w) * _window_count(col, grid_w, w)
            o_ref[...] = (s / cnt - x).astype(o_ref.dtype)


def _pool_delta(h, n_seq, seq_len, grid_w):
    rows, d = h.shape
    group = d // len(POOL_WINDOWS)
    ct = min(POOL_COLS, group)
    assert group % ct == 0
    if grid_w is not None:
        assert grid_w & (grid_w - 1) == 0 and seq_len % grid_w == 0
    kern = functools.partial(_pool_kernel, tiles_per_group=group // ct, grid_w=grid_w,
                             grid_rows=None if grid_w is None else seq_len // grid_w)
    return pl.pallas_call(
        kern,
        grid=(n_seq, d // ct),
        in_specs=[pl.BlockSpec((seq_len, ct), lambda b, j: (b, j))],
        out_specs=pl.BlockSpec((seq_len, ct), lambda b, j: (b, j)),
        out_shape=jax.ShapeDtypeStruct((rows, d), BF16),
        compiler_params=_params("parallel", "parallel"),
        name="pool",
    )(h)


def _halo_specs(st, d):
    per = st.tm // HALO
    last = st.rows // HALO - 1
    return [pl.BlockSpec((st.tm, d), lambda i, j: (i, 0)),
            pl.BlockSpec((HALO, d), lambda i, j: (jnp.maximum(i * per - 1, 0), 0)),
            pl.BlockSpec((HALO, d), lambda i, j: (jnp.minimum((i + 1) * per, last), 0))]


def _w_spec(layer, d, tn, offset=0):
    return pl.BlockSpec((None, d, tn), lambda i, j: (layer, 0, j + offset))


def _fill_lhs(lhs_ref, hm_ref, hp_ref, hn_ref, tiles_per_seq):
    tm = hm_ref.shape[0]
    i = pl.program_id(0) % tiles_per_seq
    lhs_ref[0:tm, :] = hm_ref[...]
    r = lax.broadcasted_iota(jnp.int32, hp_ref.shape, 0)
    nxt = jnp.where(i == tiles_per_seq - 1, jnp.zeros_like(hn_ref[...]), hn_ref[...])
    prv = jnp.where(i == 0, jnp.zeros_like(hp_ref[...]), hp_ref[...])
    lhs_ref[tm:tm + HALO, :] = jnp.where(r < HALO // 2, nxt, prv)


def _conv3(a_ext, cw, tm):
    n = a_ext.shape[0]
    dn = pltpu.roll(a_ext, 1, 0)[0:tm]
    up = pltpu.roll(a_ext, n - 1, 0)[0:tm]
    return cw[0:1] * dn + cw[1:2] * a_ext[0:tm] + cw[2:3] * up


def _ffn_in_kernel(hm_ref, hp_ref, hn_ref, wa_ref, wu_ref, cw_ref, o_ref, lhs_ref, *, tiles_per_seq):
    tm = hm_ref.shape[0]

    @pl.when(pl.program_id(1) == 0)
    def _():
        _fill_lhs(lhs_ref, hm_ref, hp_ref, hn_ref, tiles_per_seq)

    a = jnp.dot(lhs_ref[...], wa_ref[...], preferred_element_type=F32)
    u = jnp.dot(lhs_ref[0:tm, :], wu_ref[...], preferred_element_type=F32)
    o_ref[...] = (_silu(_conv3(a, cw_ref[...], tm)) * u).astype(o_ref.dtype)


def _ffn_in(h, w_in, conv, layer, st):
    rows, d = h.shape
    f = conv.shape[-1]
    tn = _col_tile(f)
    nj = f // tn
    kern = functools.partial(_ffn_in_kernel, tiles_per_seq=st.tiles_per_seq)
    return pl.pallas_call(
        kern,
        grid=(st.n_tiles, nj),
        in_specs=_halo_specs(st, d) + [_w_spec(layer, d, tn), _w_spec(layer, d, tn, nj),
                                       pl.BlockSpec((None, 3, tn), lambda i, j: (layer, 0, j))],
        out_specs=pl.BlockSpec((st.tm, tn), lambda i, j: (i, j)),
        out_shape=jax.ShapeDtypeStruct((rows, f), BF16),
        scratch_shapes=[pltpu.VMEM((st.tm + HALO, d), BF16)],
        compiler_params=_params("parallel", "arbitrary"),
        name="ffn_in",
    )(h, h, h, w_in, w_in, conv)


def _sc_in_kernel(hm_ref, hp_ref, hn_ref, wb_ref, wc_ref, wv_ref, cw_ref, o_ref, lhs_ref, *, tiles_per_seq):
    tm = hm_ref.shape[0]

    @pl.when(pl.program_id(1) == 0)
    def _():
        _fill_lhs(lhs_ref, hm_ref, hp_ref, hn_ref, tiles_per_seq)

    lhs = lhs_ref[...]
    cg = jnp.dot(lhs, wc_ref[...], preferred_element_type=F32)
    v = jnp.dot(lhs, wv_ref[...], preferred_element_type=F32)
    bg = jnp.dot(lhs_ref[0:tm, :], wb_ref[...], preferred_element_type=F32)
    o_ref[...] = (bg * _conv3(cg * v, cw_ref[...], tm)).astype(o_ref.dtype)


def _sc_in(h, w_in, conv, layer, st):
    rows, d = h.shape
    n = conv.shape[-1]
    tn = _col_tile(n)
    nj = n // tn
    kern = functools.partial(_sc_in_kernel, tiles_per_seq=st.tiles_per_seq)
    return pl.pallas_call(
        kern,
        grid=(st.n_tiles, nj),
        in_specs=_halo_specs(st, d) + [_w_spec(layer, d, tn), _w_spec(layer, d, tn, nj),
                                       _w_spec(layer, d, tn, 2 * nj),
                                       pl.BlockSpec((None, 3, tn), lambda i, j: (layer, 0, j))],
        out_specs=pl.BlockSpec((st.tm, tn), lambda i, j: (i, j)),
        out_shape=jax.ShapeDtypeStruct((rows, n), BF16),
        scratch_shapes=[pltpu.VMEM((st.tm + HALO, d), BF16)],
        compiler_params=_params("parallel", "arbitrary"),
        name="sc_in",
    )(h, h, h, w_in, w_in, w_in, conv)


def _block_scan(x, pos, block, reverse):
    n = x.shape[0]
    s = 1
    while s < block:
        if reverse:
            x = x + jnp.where(pos + s < block, pltpu.roll(x, n - s, 0), 0.0)
        else:
            x = x + jnp.where(pos >= s, pltpu.roll(x, s, 0), 0.0)
        s *= 2
    return x


def _gdn_in_kernel(hm_ref, hp_ref, hn_ref, w_ref, wab_ref, cw_ref, ab_ref, o_ref, gb_ref, gbt_ref, lhs_ref,
                   *, tiles_per_seq, n_qk_tiles, n_conv_tiles, n_heads, sub):
    tm = hm_ref.shape[0]
    tn = w_ref.shape[1]
    j = pl.program_id(1)

    @pl.when(j == 0)
    def _():
        _fill_lhs(lhs_ref, hm_ref, hp_ref, hn_ref, tiles_per_seq)
        lane = lax.broadcasted_iota(jnp.int32, (1, LANES), 1)
        ab = jnp.dot(lhs_ref[0:tm, :], wab_ref[...], preferred_element_type=F32)
        ab = jnp.where(lane < 2 * N_DIRS * n_heads, ab, 0.0)
        p = ab_ref[...]
        x = ab + p[1:2]
        log_alpha = -jnp.exp(p[0:1]) * (jnp.maximum(x, 0.0) + jnp.log1p(jnp.exp(-jnp.abs(x))))
        beta = jax.nn.sigmoid(ab)
        pos = lax.broadcasted_iota(jnp.int32, (tm, 1), 0) & (SCAN_BLOCK - 1)
        g = jnp.where(lane < n_heads, _block_scan(log_alpha, pos, SCAN_BLOCK, False),
                      _block_scan(log_alpha, pos, SCAN_BLOCK, True))
        gb = jnp.where(lane < N_DIRS * n_heads, g, beta)
        gb_ref[...] = gb
        gbt_ref[...] = gb.T

    scale = jnp.where(j < n_qk_tiles // 2, HEAD_DIM ** -0.5, 1.0).astype(F32)
    for c0 in range(0, tn, sub):
        a = jnp.dot(lhs_ref[...], w_ref[:, c0:c0 + sub], preferred_element_type=F32)
        x = jnp.where(j < n_conv_tiles, _silu(_conv3(a, cw_ref[:, c0:c0 + sub], tm)), a[0:tm])
        for hh in range(sub // HEAD_DIM):
            xh = x[:, hh * HEAD_DIM:(hh + 1) * HEAD_DIM]
            inv = lax.rsqrt(jnp.sum(xh * xh, axis=-1, keepdims=True) + NORM_EPS) * scale
            o_ref[c0 // HEAD_DIM + hh] = (xh * jnp.where(j < n_qk_tiles, inv, 1.0)).astype(o_ref.dtype)


def _gdn_in(h, w_in, conv, layer, ab_par, st, n_heads):
    rows, d = h.shape
    n = 4 * d
    assert w_in.shape[-1] - n <= LANES
    tn = _col_tile(d, GDN_COL_TILE)
    sub = min(tn, 2 * HEAD_DIM)
    nj = n // tn
    n_conv = conv.shape[-1] // tn
    kern = functools.partial(_gdn_in_kernel, tiles_per_seq=st.tiles_per_seq, n_qk_tiles=2 * d // tn,
                             n_conv_tiles=n_conv, n_heads=n_heads, sub=sub)
    hpt = tn // HEAD_DIM
    return pl.pallas_call(
        kern,
        grid=(st.n_tiles, nj),
        in_specs=_halo_specs(st, d) + [
            _w_spec(layer, d, tn),
            pl.BlockSpec((None, d, LANES), lambda i, j: (layer, 0, n // LANES)),
            pl.BlockSpec((None, 3, tn), lambda i, j: (layer, 0, jnp.minimum(j, n_conv - 1))),
            pl.BlockSpec((2, LANES), lambda i, j: (0, 0))],
        out_specs=[pl.BlockSpec((hpt, st.tm, HEAD_DIM), lambda i, j: (j, i, 0)),
                   pl.BlockSpec((st.tm, LANES), lambda i, j: (i, 0)),
                   pl.BlockSpec((LANES, st.tm), lambda i, j: (0, i))],
        out_shape=[jax.ShapeDtypeStruct((n // HEAD_DIM, rows, HEAD_DIM), BF16),
                   jax.ShapeDtypeStruct((rows, LANES), F32),
                   jax.ShapeDtypeStruct((LANES, rows), F32)],
        scratch_shapes=[pltpu.VMEM((st.tm + HALO, d), BF16)],
        compiler_params=_params("parallel", "arbitrary"),
        name="gdn_in",
    )(h, h, h, w_in, w_in, conv, ab_par)


def _each(f, *lists):
    return [f(*xs) for xs in zip(*lists)]


def _bdot(a, b):
    return _each(lambda x, y: jnp.dot(x.astype(BF16), y.astype(BF16), preferred_element_type=F32), a, b)


def _unit_tri_solve(neg_l, rhs, ri, ci, upper):
    c = neg_l[0].shape[0]
    base = 8

    def same_block(width):
        sh = width.bit_length() - 1
        return (ri >> sh) == (ci >> sh)

    diag = same_block(base)
    n0 = [jnp.where(diag, m, 0.0) for m in neg_l]
    p2 = _bdot(n0, n0)
    t = _bdot([jnp.concatenate([a, b], axis=0) for a, b in zip(n0, p2)], p2)
    x = _each(lambda a, b, tt: a + b + tt[0:c], n0, p2, t)
    p4 = [tt[c:2 * c] for tt in t]
    x = _each(lambda xx, p, u: xx + p + u, x, p4, _bdot(x, p4))
    size = base
    while size < c:
        pair = jnp.logical_and(same_block(2 * size), jnp.logical_not(same_block(size)))
        cm = [jnp.where(pair, m, 0.0) for m in neg_l]
        blocks = [range(0 if up else 1, c // size, 2) for up in upper]
        pick = lambda m, bl: jnp.concatenate([m[b * size:(b + 1) * size] for b in bl], axis=0)
        cm_h = _each(pick, cm, blocks)
        a_h = _each(lambda m, u: m + u, cm_h, _bdot(_each(pick, x, blocks), cm))
        a_h = _each(lambda aa, u: aa + u, a_h, _bdot(a_h, x))
        zero = jnp.zeros((size, c), F32)

        def spread(aa, bl):
            slabs = [aa[k * size:(k + 1) * size] for k in range(len(bl))]
            return jnp.concatenate([s for k in range(len(bl)) for s in
                                    ((slabs[k], zero) if bl[0] == 0 else (zero, slabs[k]))], axis=0)

        x = _each(lambda xx, aa, bl: xx + spread(aa, bl), x, a_h, blocks)
        size *= 2
    return _each(lambda r, u: r + u, rhs, _bdot(x, rhs))


def _delta_blocks(q, k, v, gcol, bcol, grow, s, reverse):
    c = q[0].shape[0]
    ri = lax.broadcasted_iota(jnp.int32, (c, c), 0)
    ci = lax.broadcasted_iota(jnp.int32, (c, c), 1)
    seen = [(ri <= ci) if r else (ri >= ci) for r in reverse]
    before = [(ri < ci) if r else (ri > ci) for r in reverse]
    g_end = [g[0:1] if r else g[c - 1:c] for g, r in zip(gcol, reverse)]
    kf = [x.astype(F32) for x in k]
    kb = _each(lambda x, b: x * b, kf, bcol)
    eg = [jnp.exp(g) for g in gcol]
    kq = _each(lambda x, y, z: lax.dot_general(jnp.concatenate([x.astype(BF16), y], axis=0), z,
                                               (((1,), (1,)), ((), ())), preferred_element_type=F32), kb, q, k)
    gam = _each(lambda m, gc, gr: jnp.exp(jnp.where(m, gc - gr, -jnp.inf)), seen, gcol, grow)
    neg_l = _each(lambda m, x, gm: jnp.where(m, -(x[0:c] * gm), 0.0), before, kq, gam)
    qk = _each(lambda x, gm: x[c:2 * c] * gm, kq, gam)
    rhs = _each(lambda x, b, y, e: jnp.concatenate([x.astype(F32) * b, y * e], axis=1), v, bcol, kb, eg)
    y = _unit_tri_solve(neg_l, rhs, ri, ci, reverse)
    q_dec = _each(lambda x, e: x.astype(F32) * e, q, eg)
    k_dec = _each(lambda x, ge, g: x * jnp.exp(ge - g), kf, g_end, gcol)
    vs = _bdot(_each(lambda yy, qd: jnp.concatenate([yy[:, HEAD_DIM:2 * HEAD_DIM], qd], axis=0), y, q_dec), s)
    v_new = _each(lambda yy, u: (yy[:, 0:HEAD_DIM] - u[0:c]).astype(BF16), y, vs)
    o = _each(lambda u, w: u[c:2 * c] + w, vs, _bdot(qk, v_new))
    kv = _each(lambda x, w: lax.dot_general(x.astype(BF16), w, (((0,), (0,)), ((), ())),
                                            preferred_element_type=F32), k_dec, v_new)
    s_new = _each(lambda ss, ge, u: ss * jnp.exp(ge) + u, s, g_end, kv)
    return o, s_new


def _gdn_core_kernel(qf_ref, kf_ref, vf_ref, qb_ref, kb_ref, vb_ref, gbf_ref, gbb_ref, gtf_ref, gtb_ref,
                     s0_ref, of_ref, ob_ref, sout_ref, s_ref, *, n_heads):
    n = pl.program_id(2)
    heads = qf_ref.shape[0]
    h0 = pl.program_id(1) * heads

    @pl.when(n == 0)
    def _():
        s_ref[...] = s0_ref[...]

    lane = lax.broadcasted_iota(jnp.int32, (1, LANES), 1)
    dirs = ((qf_ref, kf_ref, vf_ref, gbf_ref, gtf_ref, of_ref, False),
            (qb_ref, kb_ref, vb_ref, gbb_ref, gtb_ref, ob_ref, True))
    q, k, v, gcol, bcol, grow, s, reverse, dest = [], [], [], [], [], [], [], [], []
    for d, (q_ref, k_ref, v_ref, gb_ref, gt_ref, o_ref, rev) in enumerate(dirs):
        gb = gb_ref[...]
        for hh in range(heads):
            col = d * n_heads + h0 + hh
            gcol.append(jnp.sum(jnp.where(lane == col, gb, 0.0), axis=1, keepdims=True))
            bcol.append(jnp.sum(jnp.where(lane == col + N_DIRS * n_heads, gb, 0.0), axis=1, keepdims=True))
            grow.append(gt_ref[pl.ds(col, 1), :])
            q.append(q_ref[hh])
            k.append(k_ref[hh])
            v.append(v_ref[hh])
            s.append(s_ref[hh, d])
            reverse.append(rev)
            dest.append((o_ref, hh, d))
    o, s_new = _delta_blocks(q, k, v, gcol, bcol, grow, s, reverse)
    for (o_ref, hh, d), oo, ss in zip(dest, o, s_new):
        o_ref[hh] = oo.astype(o_ref.dtype)
        s_ref[hh, d] = ss

    @pl.when(n == pl.num_programs(2) - 1)
    def _():
        sout_ref[...] = s_ref[...]


def _gdn_core(qkvg, gb, gbt, s0, n_seq, seq_len, n_heads, o_dtype):
    rows = n_seq * seq_len
    c = SCAN_BLOCK
    assert seq_len % c == 0
    nc = seq_len // c
    g = min(CORE_HEADS, n_heads)
    assert n_heads % g == 0
    ng = n_heads // g

    def fwd(b, n):
        return b * nc + n

    def bwd(b, n):
        return b * nc + nc - 1 - n

    def head_spec(part, blk):
        return pl.BlockSpec((g, c, HEAD_DIM), lambda b, hg, n: (part * ng + hg, blk(b, n), 0))

    state_spec = pl.BlockSpec((None, g, N_DIRS, HEAD_DIM, HEAD_DIM), lambda b, hg, n: (b, hg, 0, 0, 0))
    kern = functools.partial(_gdn_core_kernel, n_heads=n_heads)
    return pl.pallas_call(
        kern,
        grid=(n_seq, ng, nc),
        in_specs=[head_spec(0, fwd), head_spec(1, fwd), head_spec(2, fwd),
                  head_spec(0, bwd), head_spec(1, bwd), head_spec(2, bwd),
                  pl.BlockSpec((c, LANES), lambda b, hg, n: (fwd(b, n), 0)),
                  pl.BlockSpec((c, LANES), lambda b, hg, n: (bwd(b, n), 0)),
                  pl.BlockSpec((LANES, c), lambda b, hg, n: (0, fwd(b, n))),
                  pl.BlockSpec((LANES, c), lambda b, hg, n: (0, bwd(b, n))),
                  state_spec],
        out_specs=[pl.BlockSpec((g, c, HEAD_DIM), lambda b, hg, n: (hg, fwd(b, n), 0)),
                   pl.BlockSpec((g, c, HEAD_DIM), lambda b, hg, n: (hg, bwd(b, n), 0)),
                   state_spec],
        out_shape=[jax.ShapeDtypeStruct((n_heads, rows, HEAD_DIM), o_dtype),
                   jax.ShapeDtypeStruct((n_heads, rows, HEAD_DIM), o_dtype),
                   jax.ShapeDtypeStruct(s0.shape, F32)],
        scratch_shapes=[pltpu.VMEM((g, N_DIRS, HEAD_DIM, HEAD_DIM), F32)],
        compiler_params=_params("parallel", "parallel", "arbitrary"),
        name="gdn_core",
    )(qkvg, qkvg, qkvg, qkvg, qkvg, qkvg, gb, gb, gbt, gbt, s0)


def _gdn_combine_kernel(of_ref, ob_ref, gate_ref, g_ref, o_ref):
    for hh in range(of_ref.shape[0]):
        o = of_ref[hh].astype(F32) + ob_ref[hh].astype(F32)
        on = _rmsnorm(o, g_ref[...])
        o_ref[:, hh * HEAD_DIM:(hh + 1) * HEAD_DIM] = (on * _silu(gate_ref[hh].astype(F32))).astype(o_ref.dtype)


def _gdn_combine(o_f, o_b, qkvg, norm_g, tm):
    n_heads, rows, _ = o_f.shape
    g = min(CORE_HEADS, n_heads)
    ng = n_heads // g
    return pl.pallas_call(
        _gdn_combine_kernel,
        grid=(rows // tm, ng),
        in_specs=[pl.BlockSpec((g, tm, HEAD_DIM), lambda i, j: (j, i, 0)),
                  pl.BlockSpec((g, tm, HEAD_DIM), lambda i, j: (j, i, 0)),
                  pl.BlockSpec((g, tm, HEAD_DIM), lambda i, j: (3 * ng + j, i, 0)),
                  pl.BlockSpec((1, HEAD_DIM), lambda i, j: (0, 0))],
        out_specs=pl.BlockSpec((tm, g * HEAD_DIM), lambda i, j: (i, j)),
        out_shape=jax.ShapeDtypeStruct((rows, n_heads * HEAD_DIM), BF16),
        compiler_params=_params("parallel", "parallel"),
        name="gdn_combine",
    )(o_f, o_b, qkvg, norm_g.reshape(1, HEAD_DIM))


def _mm_out_kernel(*refs, gate_row, next_rows, has_scale, emit_y):
    it = iter(refs)
    lhs_ref, w_ref, y_ref, gate_ref = next(it), next(it), next(it), next(it)
    scale_ref = next(it) if has_scale else None
    gn_ref = next(it)
    modn_ref = next(it) if next_rows is not None else None
    ynew_ref = next(it) if emit_y else None
    hn_ref, row_ref = next(it), next(it)
    j = pl.program_id(1)
    nj, _, tn = row_ref.shape

    acc = jnp.dot(lhs_ref[...], w_ref[...], preferred_element_type=F32)
    if has_scale:
        acc = acc * scale_ref[...]
    y_new = y_ref[...] + gate_ref[gate_row:gate_row + 1, :] * acc
    if emit_y:
        ynew_ref[...] = y_new
    row_ref[j] = y_new

    @pl.when(j == nj - 1)
    def _():
        ss = jnp.zeros((row_ref.shape[1], 1), F32)
        for jj in range(nj):
            yj = row_ref[jj]
            ss = ss + jnp.sum(yj * yj, axis=-1, keepdims=True)
        inv = lax.rsqrt(ss / (nj * tn) + NORM_EPS)
        for jj in range(nj):
            cols = slice(jj * tn, (jj + 1) * tn)
            hj = row_ref[jj] * inv * gn_ref[:, cols]
            if next_rows is not None:
                sh, sc = next_rows
                hj = hj * (1.0 + modn_ref[sc:sc + 1, cols]) + modn_ref[sh:sh + 1, cols]
            hn_ref[:, cols] = hj.astype(hn_ref.dtype)


def _mm_out(lhs, w, w_layer, y, modr, layer, gate_row, st, next_g, next_mod, out_dtype, col_scale=None,
            emit_y=True):
    rows, n = y.shape
    grouped = w.ndim == 4
    k = w.shape[-2]
    group_n = w.shape[-1]
    tn = _col_tile(group_n, GROUP_COL_TILE if grouped else None)
    nj = n // tn
    per_group = group_n // tn
    if grouped:
        lhs_spec = pl.BlockSpec((st.tm, k), lambda i, j: (i, j // per_group))
        w_spec = pl.BlockSpec((None, None, k, tn), lambda i, j: (w_layer, j // per_group, 0, j % per_group))
    else:
        lhs_spec = pl.BlockSpec((st.tm, k), lambda i, j: (i, 0))
        w_spec = pl.BlockSpec((None, k, tn), lambda i, j: (w_layer, 0, j))
    in_specs = [lhs_spec, w_spec,
                pl.BlockSpec((st.tm, tn), lambda i, j: (i, j)),
                st.mod_spec(layer, n, tn)]
    args = [lhs, w, y, modr]
    if col_scale is not None:
        in_specs.append(pl.BlockSpec((1, tn), lambda i, j: (0, j)))
        args.append(col_scale.reshape(1, n))
    in_specs.append(pl.BlockSpec((1, n), lambda i, j: (0, 0)))
    args.append(next_g.reshape(1, n))
    next_rows = None
    if next_mod is not None:
        in_specs.append(st.mod_spec(next_mod[0], n))
        args.append(modr)
        next_rows = (next_mod[1], next_mod[2])
    out_specs, out_shape = [], []
    if emit_y:
        out_specs.append(pl.BlockSpec((st.tm, tn), lambda i, j: (i, j)))
        out_shape.append(jax.ShapeDtypeStruct((rows, n), F32))
    out_specs.append(pl.BlockSpec((st.tm, n), lambda i, j: (i, 0)))
    out_shape.append(jax.ShapeDtypeStruct((rows, n), out_dtype))
    kern = functools.partial(_mm_out_kernel, gate_row=gate_row, next_rows=next_rows,
                             has_scale=col_scale is not None, emit_y=emit_y)
    res = pl.pallas_call(
        kern,
        grid=(st.n_tiles, nj),
        in_specs=in_specs,
        out_specs=out_specs,
        out_shape=out_shape,
        scratch_shapes=[pltpu.VMEM((nj, st.tm, tn), F32)],
        compiler_params=_params("parallel", "arbitrary"),
        name="mm_out",
    )(*args)
    return res if emit_y else (None, res[0])


def kernel(x, c, ctx, c_ctx, mod_w, mod_b, norm1_g, norm2_g, ffn_w_in, ffn_conv, ffn_w_out, pool_w, pool_scale, sc_w_in, sc_conv, sc_w_out, gdn_w_in, gdn_conv, gdn_a_log, gdn_dt_bias, gdn_norm_g, gdn_w_out, final_g):
    b, t, d = x.shape
    t_ctx = ctx.shape[1]
    depth = mod_w.shape[0]
    n_mix = 3
    n_heads = d // HEAD_DIM
    assert 2 * N_DIRS * n_heads <= LANES
    gdn_layers = [i for i in range(depth) if i % n_mix == 2]
    last_ctx = gdn_layers[-1] if gdn_layers else -1

    lat = _Stream(b, t, min(ROW_TILE, t))
    cst = _Stream(b, t_ctx, min(ROW_TILE, t_ctx), ctx_row=b)
    lat.proj = _Stream(b, t, min(IN_ROW_TILE, t))
    cst.proj = _Stream(b, t_ctx, min(IN_ROW_TILE, t_ctx), ctx_row=b)
    modr = _mod_vectors(c, c_ctx, mod_w, mod_b)

    y = x.reshape(b * t, d)
    yc = ctx.reshape(b * t_ctx, d)

    def h_dtype(i):
        return F32 if i % n_mix == 0 else BF16

    h = _norm_mod(y, modr, 0, norm1_g[0], lat, h_dtype(0))
    hc = _norm_mod(yc, modr, 0, norm1_g[0], cst, h_dtype(0)) if last_ctx >= 0 else None

    ffn_w_in, ffn_w_out, pool_w, sc_w_in, sc_w_out, gdn_w_in, gdn_w_out = (
        w.astype(BF16) for w in (ffn_w_in, ffn_w_out, pool_w, sc_w_in, sc_w_out, gdn_w_in, gdn_w_out))

    out = None
    for i in range(depth):
        kind, jx = i % n_mix, i // n_mix
        ctx_stream = i < last_ctx
        last = i == depth - 1
        streams = [(lat, h, y, t, GRID_W)]
        if ctx_stream:
            streams.append((cst, hc, yc, t_ctx, None))

        if kind == 2:
            ab_par = jnp.pad(jnp.stack([gdn_a_log[jx].reshape(-1), gdn_dt_bias[jx].reshape(-1)]),
                             ((0, 0), (0, LANES - N_DIRS * n_heads)))
            qc, gbc, gbtc = _gdn_in(hc, gdn_w_in, gdn_conv, jx, ab_par, cst.proj, n_heads)
            s0 = jnp.zeros((b, n_heads, N_DIRS, HEAD_DIM, HEAD_DIM), F32)
            ocf, ocb, s_ctx = _gdn_core(qc, gbc, gbtc, s0, b, t_ctx, n_heads, BF16)
            ql, gbl, gbtl = _gdn_in(h, gdn_w_in, gdn_conv, jx, ab_par, lat.proj, n_heads)
            o_f, o_b, _ = _gdn_core(ql, gbl, gbtl, s_ctx, b, t, n_heads, F32)
            mixed = [(_gdn_combine(o_f, o_b, ql, gdn_norm_g[jx], lat.tm), gdn_w_out, None)]
            if ctx_stream:
                mixed.append((_gdn_combine(ocf, ocb, qc, gdn_norm_g[jx], cst.tm), gdn_w_out, None))
        elif kind == 1:
            mixed = [(_sc_in(hs, sc_w_in, sc_conv, jx, st), sc_w_out, None) for st, hs, _, _, _ in streams]
        else:
            mixed = [(_pool_delta(hs, b, sl, gw), pool_w, pool_scale[jx]) for st, hs, _, sl, gw in streams]

        new = []
        for (st, _, ys, _, _), (m_lhs, w_out, scale) in zip(streams, mixed):
            y1, h2 = _mm_out(m_lhs, w_out, jx, ys, modr, i, 2, st, norm2_g[i], (i, 3, 4), BF16, col_scale=scale)
            f = _ffn_in(h2, ffn_w_in, ffn_conv, i, st.proj)
            if last:
                _, hn = _mm_out(f, ffn_w_out, i, y1, modr, i, 5, st, final_g, None, F32, emit_y=False)
                y2 = None
            else:
                y2, hn = _mm_out(f, ffn_w_out, i, y1, modr, i, 5, st, norm1_g[i + 1], (i + 1, 0, 1),
                                 h_dtype(i + 1))
            new.append((y2, hn))
        y, h = new[0]
        if ctx_stream:
            yc, hc = new[1]
        if last:
            out = h
    return out.reshape(b, t, d).astype(x.dtype)
```

```python
import functools
import math

import jax
import jax.numpy as jnp
from jax import lax
from jax.experimental import pallas as pl
from jax.experimental.pallas import tpu as pltpu

F32 = jnp.float32
BF16 = jnp.bfloat16

GRID_W = 64
POOL_WINDOWS = (2, 4, 8, 16)
HEAD_DIM = 128
N_DIRS = 2
NORM_EPS = 1e-6
SCAN_BLOCK = 128
HALO = 16
LANES = 128
VMEM_LIMIT_BYTES = 56 * 1024 * 1024
ROW_TILE = 512
IN_ROW_TILE = 1024
COL_TILE = 512
GDN_COL_TILE = 512
WIDE_COL_TILE = 1024
WIDE_COL_MAX_K = 1024
DOT_ROWS = 256
POOL_COLS = 256
CORE_HEADS = 16


def _params(*sem):
    return pltpu.CompilerParams(dimension_semantics=sem, vmem_limit_bytes=VMEM_LIMIT_BYTES)


def _col_tile(n, cap=None):
    cap = COL_TILE if cap is None else cap
    for t in (cap, 512, 256, 128):
        if t <= cap and n % t == 0:
            return t
    raise ValueError(f"no column tile for {n}")


def _silu(x):
    return x * jax.nn.sigmoid(x)


def _rmsnorm(y, g):
    return y * lax.rsqrt(jnp.mean(y * y, axis=-1, keepdims=True) + NORM_EPS) * g


def _mod_kernel(c_ref, w_ref, b_ref, o_ref):
    s = _silu(c_ref[...]).astype(BF16)
    o_ref[...] = jnp.dot(s, w_ref[...].astype(BF16), preferred_element_type=F32) + b_ref[...]


def _mod_vectors(c, c_ctx, mod_w, mod_b):
    depth, d, d6 = mod_w.shape
    b = c.shape[0]
    r = -(-(b + 1) // 16) * 16
    cc = jnp.zeros((r, d), F32).at[:b].set(c).at[b].set(c_ctx)
    tn = _col_tile(d6)
    out = pl.pallas_call(
        _mod_kernel,
        grid=(depth, d6 // tn),
        in_specs=[pl.BlockSpec((r, d), lambda l, j: (0, 0)),
                  pl.BlockSpec((None, d, tn), lambda l, j: (l, 0, j)),
                  pl.BlockSpec((None, 1, tn), lambda l, j: (l, 0, j))],
        out_specs=pl.BlockSpec((None, r, tn), lambda l, j: (l, 0, j)),
        out_shape=jax.ShapeDtypeStruct((depth, r, d6), F32),
        compiler_params=_params("parallel", "parallel"),
        name="mod",
    )(cc, mod_w, mod_b.reshape(depth, 1, d6))
    return out.reshape(depth, r, 6, d)


class _Stream:
    def __init__(self, n_seq, seq_len, tm, ctx_row=None):
        assert seq_len % tm == 0 and tm % HALO == 0
        self.n_seq, self.seq_len, self.tm = n_seq, seq_len, tm
        self.tiles_per_seq = seq_len // tm
        self.rows = n_seq * seq_len
        self.n_tiles = self.rows // tm
        self.ctx_row = ctx_row

    def mod_row(self, i):
        return i // self.tiles_per_seq if self.ctx_row is None else self.ctx_row

    def mod_spec(self, layer, d, cols=None):
        if cols is None:
            return pl.BlockSpec((None, None, 6, d), lambda i, j: (layer, self.mod_row(i), 0, 0))
        return pl.BlockSpec((None, None, 6, cols), lambda i, j: (layer, self.mod_row(i), 0, j))


def _norm_kernel(y_ref, mod_ref, g_ref, o_ref):
    m = mod_ref[...]
    h = _rmsnorm(y_ref[...], g_ref[...]) * (1.0 + m[1:2]) + m[0:1]
    o_ref[...] = h.astype(o_ref.dtype)


def _norm_mod(y, modr, layer, g, st, out_dtype):
    rows, d = y.shape
    return pl.pallas_call(
        _norm_kernel,
        grid=(st.n_tiles, 1),
        in_specs=[pl.BlockSpec((st.tm, d), lambda i, j: (i, 0)),
                  st.mod_spec(layer, d),
                  pl.BlockSpec((1, d), lambda i, j: (0, 0))],
        out_specs=pl.BlockSpec((st.tm, d), lambda i, j: (i, 0)),
        out_shape=jax.ShapeDtypeStruct((rows, d), out_dtype),
        compiler_params=_params("parallel", "arbitrary"),
        name="norm_mod",
    )(y, modr, g.reshape(1, d))


def _window_sum(x, pos, n, unit, w):
    tot = x.shape[0]
    fwd, bwd = w - w // 2, w // 2
    assert fwd & (fwd - 1) == 0 and bwd & (bwd - 1) == 0 and bwd >= 1
    a, s = x, 1
    while s < fwd:
        a = a + jnp.where(pos + s < n, pltpu.roll(a, tot - s * unit, 0), 0.0)
        s *= 2
    b, s = jnp.where(pos >= 1, pltpu.roll(x, unit, 0), 0.0), 1
    while s < bwd:
        b = b + jnp.where(pos - s >= 0, pltpu.roll(b, s * unit, 0), 0.0)
        s *= 2
    return a + b


def _window_count(pos, n, w):
    return (jnp.minimum(pos + (w - w // 2), n) - jnp.maximum(pos - w // 2, 0)).astype(F32)


def _pool_kernel(h_ref, o_ref, *, tiles_per_group, grid_w, grid_rows):
    group = pl.program_id(1) // tiles_per_group
    t = h_ref.shape[0]
    tok = lax.broadcasted_iota(jnp.int32, (t, 1), 0)
    for gi, w in enumerate(POOL_WINDOWS):
        @pl.when(group == gi)
        def _(w=w):
            x = h_ref[...]
            if grid_rows is None:
                s = _window_sum(x, tok, t, 1, w)
                cnt = _window_count(tok, t, w)
            else:
                shift = grid_w.bit_length() - 1
                col, row = tok & (grid_w - 1), tok >> shift
                s = _window_sum(x, row, grid_rows, grid_w, w)
                s = _window_sum(s, col, grid_w, 1, w)
                cnt = _window_count(row, grid_rows, w) * _window_count(col, grid_w, w)
            o_ref[...] = (s / cnt - x).astype(o_ref.dtype)


def _pool_delta(h, n_seq, seq_len, grid_w):
    rows, d = h.shape
    group = d // len(POOL_WINDOWS)
    ct = min(POOL_COLS, group)
    assert group % ct == 0
    if grid_w is not None:
        assert grid_w & (grid_w - 1) == 0 and seq_len % grid_w == 0
    kern = functools.partial(_pool_kernel, tiles_per_group=group // ct, grid_w=grid_w,
                             grid_rows=None if grid_w is None else seq_len // grid_w)
    return pl.pallas_call(
        kern,
        grid=(n_seq, d // ct),
        in_specs=[pl.BlockSpec((seq_len, ct), lambda b, j: (b, j))],
        out_specs=pl.BlockSpec((seq_len, ct), lambda b, j: (b, j)),
        out_shape=jax.ShapeDtypeStruct((rows, d), BF16),
        compiler_params=_params("parallel", "parallel"),
        name="pool",
    )(h)


def _halo_specs(st, d):
    per = st.tm // HALO
    last = st.rows // HALO - 1
    return [pl.BlockSpec((st.tm, d), lambda i, j: (i, 0)),
            pl.BlockSpec((HALO, d), lambda i, j: (jnp.maximum(i * per - 1, 0), 0)),
            pl.BlockSpec((HALO, d), lambda i, j: (jnp.minimum((i + 1) * per, last), 0))]


def _w_spec(layer, d, tn, offset=0):
    return pl.BlockSpec((None, d, tn), lambda i, j: (layer, 0, j + offset))


def _fill_lhs(lhs_ref, hm_ref, hp_ref, hn_ref, tiles_per_seq):
    tm = hm_ref.shape[0]
    i = pl.program_id(0) % tiles_per_seq
    r = lax.broadcasted_iota(jnp.int32, hp_ref.shape, 0)
    nxt = jnp.where(i == tiles_per_seq - 1, jnp.zeros_like(hn_ref[...]), hn_ref[...])
    prv = jnp.where(i == 0, jnp.zeros_like(hp_ref[...]), hp_ref[...])
    lhs_ref[0:HALO, :] = jnp.where(r < HALO // 2, nxt, prv)
    lhs_ref[HALO:HALO + tm, :] = hm_ref[...]


def _dot_row_chunks(lhs_ref, w, first=0):
    n = lhs_ref.shape[0]
    bounds = [first] + list(range(first + (n - first) % DOT_ROWS + DOT_ROWS, n, DOT_ROWS)) + [n]
    parts = [jnp.dot(lhs_ref[r0:r1, :], w, preferred_element_type=F32) for r0, r1 in zip(bounds, bounds[1:])]
    return parts[0] if len(parts) == 1 else jnp.concatenate(parts, axis=0)


def _conv3(a_ext, cw, tm):
    n = a_ext.shape[0]
    dn = pltpu.roll(a_ext, 1, 0)[HALO:HALO + tm]
    up = pltpu.roll(a_ext, n - 1, 0)[HALO:HALO + tm]
    return cw[0:1] * dn + cw[1:2] * a_ext[HALO:HALO + tm] + cw[2:3] * up


def _ffn_in_kernel(hm_ref, hp_ref, hn_ref, wa_ref, wu_ref, cw_ref, o_ref, lhs_ref, *, tiles_per_seq):
    tm = hm_ref.shape[0]

    @pl.when(pl.program_id(1) == 0)
    def _():
        _fill_lhs(lhs_ref, hm_ref, hp_ref, hn_ref, tiles_per_seq)

    a = jnp.dot(lhs_ref[...], wa_ref[...], preferred_element_type=F32)
    u = jnp.dot(lhs_ref[HALO:HALO + tm, :], wu_ref[...], preferred_element_type=F32)
    o_ref[...] = (_silu(_conv3(a, cw_ref[...], tm)) * u).astype(o_ref.dtype)


def _ffn_in(h, w_in, conv, layer, st):
    rows, d = h.shape
    f = conv.shape[-1]
    tn = _col_tile(f)
    nj = f // tn
    kern = functools.partial(_ffn_in_kernel, tiles_per_seq=st.tiles_per_seq)
    return pl.pallas_call(
        kern,
        grid=(st.n_tiles, nj),
        in_specs=_halo_specs(st, d) + [_w_spec(layer, d, tn), _w_spec(layer, d, tn, nj),
                                       pl.BlockSpec((None, 3, tn), lambda i, j: (layer, 0, j))],
        out_specs=pl.BlockSpec((st.tm, tn), lambda i, j: (i, j)),
        out_shape=jax.ShapeDtypeStruct((rows, f), BF16),
        scratch_shapes=[pltpu.VMEM((st.tm + HALO, d), BF16)],
        compiler_params=_params("parallel", "arbitrary"),
        name="ffn_in",
    )(h, h, h, w_in, w_in, conv)


def _sc_in_kernel(hm_ref, hp_ref, hn_ref, wb_ref, wc_ref, wv_ref, cw_ref, o_ref, lhs_ref, *, tiles_per_seq):
    tm = hm_ref.shape[0]

    @pl.when(pl.program_id(1) == 0)
    def _():
        _fill_lhs(lhs_ref, hm_ref, hp_ref, hn_ref, tiles_per_seq)

    lhs = lhs_ref[...]
    cg = jnp.dot(lhs, wc_ref[...], preferred_element_type=F32)
    v = jnp.dot(lhs, wv_ref[...], preferred_element_type=F32)
    bg = jnp.dot(lhs_ref[HALO:HALO + tm, :], wb_ref[...], preferred_element_type=F32)
    o_ref[...] = (bg * _conv3(cg * v, cw_ref[...], tm)).astype(o_ref.dtype)


def _sc_in(h, w_in, conv, layer, st):
    rows, d = h.shape
    n = conv.shape[-1]
    tn = _col_tile(n)
    nj = n // tn
    kern = functools.partial(_sc_in_kernel, tiles_per_seq=st.tiles_per_seq)
    return pl.pallas_call(
        kern,
        grid=(st.n_tiles, nj),
        in_specs=_halo_specs(st, d) + [_w_spec(layer, d, tn), _w_spec(layer, d, tn, nj),
                                       _w_spec(layer, d, tn, 2 * nj),
                                       pl.BlockSpec((None, 3, tn), lambda i, j: (layer, 0, j))],
        out_specs=pl.BlockSpec((st.tm, tn), lambda i, j: (i, j)),
        out_shape=jax.ShapeDtypeStruct((rows, n), BF16),
        scratch_shapes=[pltpu.VMEM((st.tm + HALO, d), BF16)],
        compiler_params=_params("parallel", "arbitrary"),
        name="sc_in",
    )(h, h, h, w_in, w_in, w_in, conv)


def _block_scan(x, pos, block, reverse):
    n = x.shape[0]
    s = 1
    while s < block:
        if reverse:
            x = x + jnp.where(pos + s < block, pltpu.roll(x, n - s, 0), 0.0)
        else:
            x = x + jnp.where(pos >= s, pltpu.roll(x, s, 0), 0.0)
        s *= 2
    return x


def _gdn_in_kernel(hm_ref, hp_ref, hn_ref, w_ref, wab_ref, cw_ref, ab_ref, o_ref, gb_ref, gbt_ref, lhs_ref,
                   *, tiles_per_seq, n_qk_tiles, n_conv_tiles, n_heads):
    tm = hm_ref.shape[0]
    tn = w_ref.shape[1]
    j = pl.program_id(1)

    @pl.when(j == 0)
    def _():
        _fill_lhs(lhs_ref, hm_ref, hp_ref, hn_ref, tiles_per_seq)
        lane = lax.broadcasted_iota(jnp.int32, (1, LANES), 1)
        ab = jnp.dot(lhs_ref[HALO:HALO + tm, :], wab_ref[...], preferred_element_type=F32)
        ab = jnp.where(lane < 2 * N_DIRS * n_heads, ab, 0.0)
        p = ab_ref[...]
        x = ab + p[1:2]
        log_alpha = -jnp.exp(p[0:1]) * (jnp.maximum(x, 0.0) + jnp.log1p(jnp.exp(-jnp.abs(x))))
        beta = jax.nn.sigmoid(ab)
        pos = lax.broadcasted_iota(jnp.int32, (tm, 1), 0) & (SCAN_BLOCK - 1)
        g = jnp.where(lane < n_heads, _block_scan(log_alpha, pos, SCAN_BLOCK, False),
                      _block_scan(log_alpha, pos, SCAN_BLOCK, True))
        gb = jnp.where(lane < N_DIRS * n_heads, g, beta)
        gb_ref[...] = gb
        gbt_ref[...] = gb.T

    a = _dot_row_chunks(lhs_ref, w_ref[...])
    x = jnp.where(j < n_conv_tiles, _silu(_conv3(a, cw_ref[...], tm)), a[HALO:HALO + tm])
    scale = jnp.where(j < n_qk_tiles // 2, HEAD_DIM ** -0.5, 1.0).astype(F32)
    for hh in range(tn // HEAD_DIM):
        xh = x[:, hh * HEAD_DIM:(hh + 1) * HEAD_DIM]
        inv = lax.rsqrt(jnp.sum(xh * xh, axis=-1, keepdims=True) + NORM_EPS) * scale
        o_ref[hh] = (xh * jnp.where(j < n_qk_tiles, inv, 1.0)).astype(o_ref.dtype)


def _gdn_in(h, w_in, conv, layer, ab_par, st, n_heads):
    rows, d = h.shape
    n = 4 * d
    assert w_in.shape[-1] - n <= LANES
    tn = _col_tile(d, GDN_COL_TILE)
    nj = n // tn
    n_conv = conv.shape[-1] // tn
    kern = functools.partial(_gdn_in_kernel, tiles_per_seq=st.tiles_per_seq, n_qk_tiles=2 * d // tn,
                             n_conv_tiles=n_conv, n_heads=n_heads)
    hpt = tn // HEAD_DIM
    return pl.pallas_call(
        kern,
        grid=(st.n_tiles, nj),
        in_specs=_halo_specs(st, d) + [
            _w_spec(layer, d, tn),
            pl.BlockSpec((None, d, LANES), lambda i, j: (layer, 0, n // LANES)),
            pl.BlockSpec((None, 3, tn), lambda i, j: (layer, 0, jnp.minimum(j, n_conv - 1))),
            pl.BlockSpec((2, LANES), lambda i, j: (0, 0))],
        out_specs=[pl.BlockSpec((hpt, st.tm, HEAD_DIM), lambda i, j: (j, i, 0)),
                   pl.BlockSpec((st.tm, LANES), lambda i, j: (i, 0)),
                   pl.BlockSpec((LANES, st.tm), lambda i, j: (0, i))],
        out_shape=[jax.ShapeDtypeStruct((n // HEAD_DIM, rows, HEAD_DIM), BF16),
                   jax.ShapeDtypeStruct((rows, LANES), F32),
                   jax.ShapeDtypeStruct((LANES, rows), F32)],
        scratch_shapes=[pltpu.VMEM((st.tm + HALO, d), BF16)],
        compiler_params=_params("parallel", "arbitrary"),
        name="gdn_in",
    )(h, h, h, w_in, w_in, conv, ab_par)


def _each(f, *lists):
    return [f(*xs) for xs in zip(*lists)]


def _bdot(a, b):
    return _each(lambda x, y: jnp.dot(x.astype(BF16), y.astype(BF16), preferred_element_type=F32), a, b)


def _unit_tri_solve(neg_l, rhs, ri, ci, upper):
    c = neg_l[0].shape[0]
    base = 8

    def same_block(width):
        sh = width.bit_length() - 1
        return (ri >> sh) == (ci >> sh)

    diag = same_block(base)
    n0 = [jnp.where(diag, m, 0.0) for m in neg_l]
    p2 = _bdot(n0, n0)
    t = _bdot([jnp.concatenate([a, b], axis=0) for a, b in zip(n0, p2)], p2)
    x = _each(lambda a, b, tt: a + b + tt[0:c], n0, p2, t)
    p4 = [tt[c:2 * c] for tt in t]
    x = _each(lambda xx, p, u: xx + p + u, x, p4, _bdot(x, p4))
    size = base
    while size < c:
        pair = jnp.logical_and(same_block(2 * size), jnp.logical_not(same_block(size)))
        cm = [jnp.where(pair, m, 0.0) for m in neg_l]
        blocks = [range(0 if up else 1, c // size, 2) for up in upper]
        pick = lambda m, bl: jnp.concatenate([m[b * size:(b + 1) * size] for b in bl], axis=0)
        cm_h = _each(pick, cm, blocks)
        a_h = _each(lambda m, u: m + u, cm_h, _bdot(_each(pick, x, blocks), cm))
        a_h = _each(lambda aa, u: aa + u, a_h, _bdot(a_h, x))
        zero = jnp.zeros((size, c), F32)

        def spread(aa, bl):
            slabs = [aa[k * size:(k + 1) * size] for k in range(len(bl))]
            return jnp.concatenate([s for k in range(len(bl)) for s in
                                    ((slabs[k], zero) if bl[0] == 0 else (zero, slabs[k]))], axis=0)

        x = _each(lambda xx, aa, bl: xx + spread(aa, bl), x, a_h, blocks)
        size *= 2
    return _each(lambda r, u: r + u, rhs, _bdot(x, rhs))


def _delta_blocks(q, k, v, gcol, bcol, grow, s, reverse):
    c = q[0].shape[0]
    ri = lax.broadcasted_iota(jnp.int32, (c, c), 0)
    ci = lax.broadcasted_iota(jnp.int32, (c, c), 1)
    seen = [(ri <= ci) if r else (ri >= ci) for r in reverse]
    before = [(ri < ci) if r else (ri > ci) for r in reverse]
    g_end = [g[0:1] if r else g[c - 1:c] for g, r in zip(gcol, reverse)]
    kf = [x.astype(F32) for x in k]
    kb = _each(lambda x, b: x * b, kf, bcol)
    eg = [jnp.exp(g) for g in gcol]
    kq = _each(lambda x, y, z: lax.dot_general(jnp.concatenate([x.astype(BF16), y], axis=0), z,
                                               (((1,), (1,)), ((), ())), preferred_element_type=F32), kb, q, k)
    gam = _each(lambda m, gc, gr: jnp.exp(jnp.where(m, gc - gr, -jnp.inf)), seen, gcol, grow)
    neg_l = _each(lambda m, x, gm: jnp.where(m, -(x[0:c] * gm), 0.0), before, kq, gam)
    qk = _each(lambda x, gm: x[c:2 * c] * gm, kq, gam)
    rhs = _each(lambda x, b, y, e: jnp.concatenate([x.astype(F32) * b, y * e], axis=1), v, bcol, kb, eg)
    y = _unit_tri_solve(neg_l, rhs, ri, ci, reverse)
    q_dec = _each(lambda x, e: x.astype(F32) * e, q, eg)
    k_dec = _each(lambda x, ge, g: x * jnp.exp(ge - g), kf, g_end, gcol)
    vs = _bdot(_each(lambda yy, qd: jnp.concatenate([yy[:, HEAD_DIM:2 * HEAD_DIM], qd], axis=0), y, q_dec), s)
    v_new = _each(lambda yy, u: (yy[:, 0:HEAD_DIM] - u[0:c]).astype(BF16), y, vs)
    o = _each(lambda u, w: u[c:2 * c] + w, vs, _bdot(qk, v_new))
    kv = _each(lambda x, w: lax.dot_general(x.astype(BF16), w, (((0,), (0,)), ((), ())),
                                            preferred_element_type=F32), k_dec, v_new)
    s_new = _each(lambda ss, ge, u: ss * jnp.exp(ge) + u, s, g_end, kv)
    return o, s_new


def _gdn_core_kernel(qf_ref, kf_ref, vf_ref, qb_ref, kb_ref, vb_ref, gbf_ref, gbb_ref, gtf_ref, gtb_ref,
                     s0_ref, of_ref, ob_ref, sout_ref, s_ref, *, n_heads):
    n = pl.program_id(2)
    heads = qf_ref.shape[0]
    h0 = pl.program_id(1) * heads

    @pl.when(n == 0)
    def _():
        s_ref[...] = s0_ref[...]

    lane = lax.broadcasted_iota(jnp.int32, (1, LANES), 1)
    dirs = ((qf_ref, kf_ref, vf_ref, gbf_ref, gtf_ref, of_ref, False),
            (qb_ref, kb_ref, vb_ref, gbb_ref, gtb_ref, ob_ref, True))
    q, k, v, gcol, bcol, grow, s, reverse, dest = [], [], [], [], [], [], [], [], []
    for d, (q_ref, k_ref, v_ref, gb_ref, gt_ref, o_ref, rev) in enumerate(dirs):
        gb = gb_ref[...]
        for hh in range(heads):
            col = d * n_heads + h0 + hh
            gcol.append(jnp.sum(jnp.where(lane == col, gb, 0.0), axis=1, keepdims=True))
            bcol.append(jnp.sum(jnp.where(lane == col + N_DIRS * n_heads, gb, 0.0), axis=1, keepdims=True))
            grow.append(gt_ref[pl.ds(col, 1), :])
            q.append(q_ref[hh])
            k.append(k_ref[hh])
            v.append(v_ref[hh])
            s.append(s_ref[hh, d])
            reverse.append(rev)
            dest.append((o_ref, hh, d))
    o, s_new = _delta_blocks(q, k, v, gcol, bcol, grow, s, reverse)
    for (o_ref, hh, d), oo, ss in zip(dest, o, s_new):
        o_ref[hh] = oo.astype(o_ref.dtype)
        s_ref[hh, d] = ss

    @pl.when(n == pl.num_programs(2) - 1)
    def _():
        sout_ref[...] = s_ref[...]


def _gdn_core(qkvg, gb, gbt, s0, n_seq, seq_len, n_heads, o_dtype):
    rows = n_seq * seq_len
    c = SCAN_BLOCK
    assert seq_len % c == 0
    nc = seq_len // c
    g = min(CORE_HEADS, n_heads)
    assert n_heads % g == 0
    ng = n_heads // g

    def fwd(b, n):
        return b * nc + n

    def bwd(b, n):
        return b * nc + nc - 1 - n

    def head_spec(part, blk):
        return pl.BlockSpec((g, c, HEAD_DIM), lambda b, hg, n: (part * ng + hg, blk(b, n), 0))

    state_spec = pl.BlockSpec((None, g, N_DIRS, HEAD_DIM, HEAD_DIM), lambda b, hg, n: (b, hg, 0, 0, 0))
    kern = functools.partial(_gdn_core_kernel, n_heads=n_heads)
    return pl.pallas_call(
        kern,
        grid=(n_seq, ng, nc),
        in_specs=[head_spec(0, fwd), head_spec(1, fwd), head_spec(2, fwd),
                  head_spec(0, bwd), head_spec(1, bwd), head_spec(2, bwd),
                  pl.BlockSpec((c, LANES), lambda b, hg, n: (fwd(b, n), 0)),
                  pl.BlockSpec((c, LANES), lambda b, hg, n: (bwd(b, n), 0)),
                  pl.BlockSpec((LANES, c), lambda b, hg, n: (0, fwd(b, n))),
                  pl.BlockSpec((LANES, c), lambda b, hg, n: (0, bwd(b, n))),
                  state_spec],
        out_specs=[pl.BlockSpec((g, c, HEAD_DIM), lambda b, hg, n: (hg, fwd(b, n), 0)),
                   pl.BlockSpec((g, c, HEAD_DIM), lambda b, hg, n: (hg, bwd(b, n), 0)),
                   state_spec],
        out_shape=[jax.ShapeDtypeStruct((n_heads, rows, HEAD_DIM), o_dtype),
                   jax.ShapeDtypeStruct((n_heads, rows, HEAD_DIM), o_dtype),
                   jax.ShapeDtypeStruct(s0.shape, F32)],
        scratch_shapes=[pltpu.VMEM((g, N_DIRS, HEAD_DIM, HEAD_DIM), F32)],
        compiler_params=_params("parallel", "parallel", "arbitrary"),
        name="gdn_core",
    )(qkvg, qkvg, qkvg, qkvg, qkvg, qkvg, gb, gb, gbt, gbt, s0)


def _gdn_combine_kernel(of_ref, ob_ref, gate_ref, g_ref, o_ref):
    for hh in range(of_ref.shape[0]):
        o = of_ref[hh].astype(F32) + ob_ref[hh].astype(F32)
        on = _rmsnorm(o, g_ref[...])
        o_ref[:, hh * HEAD_DIM:(hh + 1) * HEAD_DIM] = (on * _silu(gate_ref[hh].astype(F32))).astype(o_ref.dtype)


def _gdn_combine(o_f, o_b, qkvg, norm_g, tm):
    n_heads, rows, _ = o_f.shape
    g = min(CORE_HEADS, n_heads)
    ng = n_heads // g
    return pl.pallas_call(
        _gdn_combine_kernel,
        grid=(rows // tm, ng),
        in_specs=[pl.BlockSpec((g, tm, HEAD_DIM), lambda i, j: (j, i, 0)),
                  pl.BlockSpec((g, tm, HEAD_DIM), lambda i, j: (j, i, 0)),
                  pl.BlockSpec((g, tm, HEAD_DIM), lambda i, j: (3 * ng + j, i, 0)),
                  pl.BlockSpec((1, HEAD_DIM), lambda i, j: (0, 0))],
        out_specs=pl.BlockSpec((tm, g * HEAD_DIM), lambda i, j: (i, j)),
        out_shape=jax.ShapeDtypeStruct((rows, n_heads * HEAD_DIM), BF16),
        compiler_params=_params("parallel", "parallel"),
        name="gdn_combine",
    )(o_f, o_b, qkvg, norm_g.reshape(1, HEAD_DIM))


def _mm_out_kernel(*refs, gate_row, next_rows, has_scale, emit_y):
    it = iter(refs)
    lhs_ref, w_ref, y_ref, gate_ref = next(it), next(it), next(it), next(it)
    scale_ref = next(it) if has_scale else None
    gn_ref = next(it)
    modn_ref = next(it) if next_rows is not None else None
    ynew_ref = next(it) if emit_y else None
    hn_ref, row_ref = next(it), next(it)
    j = pl.program_id(1)
    nj, _, tn = row_ref.shape

    acc = jnp.dot(lhs_ref[...], w_ref[...], preferred_element_type=F32)
    if has_scale:
        acc = acc * scale_ref[...]
    y_new = y_ref[...] + gate_ref[gate_row:gate_row + 1, :] * acc
    if emit_y:
        ynew_ref[...] = y_new
    row_ref[j] = y_new

    @pl.when(j == nj - 1)
    def _():
        ss = jnp.zeros((row_ref.shape[1], 1), F32)
        for jj in range(nj):
            yj = row_ref[jj]
            ss = ss + jnp.sum(yj * yj, axis=-1, keepdims=True)
        inv = lax.rsqrt(ss / (nj * tn) + NORM_EPS)
        for jj in range(nj):
            cols = slice(jj * tn, (jj + 1) * tn)
            hj = row_ref[jj] * inv * gn_ref[:, cols]
            if next_rows is not None:
                sh, sc = next_rows
                hj = hj * (1.0 + modn_ref[sc:sc + 1, cols]) + modn_ref[sh:sh + 1, cols]
            hn_ref[:, cols] = hj.astype(hn_ref.dtype)


def _mm_out(lhs, w, w_layer, y, modr, layer, gate_row, st, next_g, next_mod, out_dtype, col_scale=None,
            emit_y=True):
    rows, n = y.shape
    grouped = w.ndim == 4
    k = w.shape[-2]
    group_n = w.shape[-1]
    tn = _col_tile(group_n, WIDE_COL_TILE if k <= WIDE_COL_MAX_K else None)
    nj = n // tn
    per_group = group_n // tn
    if grouped:
        lhs_spec = pl.BlockSpec((st.tm, k), lambda i, j: (i, j // per_group))
        w_spec = pl.BlockSpec((None, None, k, tn), lambda i, j: (w_layer, j // per_group, 0, j % per_group))
    else:
        lhs_spec = pl.BlockSpec((st.tm, k), lambda i, j: (i, 0))
        w_spec = pl.BlockSpec((None, k, tn), lambda i, j: (w_layer, 0, j))
    in_specs = [lhs_spec, w_spec,
                pl.BlockSpec((st.tm, tn), lambda i, j: (i, j)),
                st.mod_spec(layer, n, tn)]
    args = [lhs, w, y, modr]
    if col_scale is not None:
        in_specs.append(pl.BlockSpec((1, tn), lambda i, j: (0, j)))
        args.append(col_scale.reshape(1, n))
    in_specs.append(pl.BlockSpec((1, n), lambda i, j: (0, 0)))
    args.append(next_g.reshape(1, n))
    next_rows = None
    if next_mod is not None:
        in_specs.append(st.mod_spec(next_mod[0], n))
        args.append(modr)
        next_rows = (next_mod[1], next_mod[2])
    out_specs, out_shape = [], []
    if emit_y:
        out_specs.append(pl.BlockSpec((st.tm, tn), lambda i, j: (i, j)))
        out_shape.append(jax.ShapeDtypeStruct((rows, n), F32))
    out_specs.append(pl.BlockSpec((st.tm, n), lambda i, j: (i, 0)))
    out_shape.append(jax.ShapeDtypeStruct((rows, n), out_dtype))
    kern = functools.partial(_mm_out_kernel, gate_row=gate_row, next_rows=next_rows,
                             has_scale=col_scale is not None, emit_y=emit_y)
    res = pl.pallas_call(
        kern,
        grid=(st.n_tiles, nj),
        in_specs=in_specs,
        out_specs=out_specs,
        out_shape=out_shape,
        scratch_shapes=[pltpu.VMEM((nj, st.tm, tn), F32)],
        compiler_params=_params("parallel", "arbitrary"),
        name="mm_out",
    )(*args)
    return res if emit_y else (None, res[0])


def kernel(x, c, ctx, c_ctx, mod_w, mod_b, norm1_g, norm2_g, ffn_w_in, ffn_conv, ffn_w_out, pool_w, pool_scale, sc_w_in, sc_conv, sc_w_out, gdn_w_in, gdn_conv, gdn_a_log, gdn_dt_bias, gdn_norm_g, gdn_w_out, final_g):
    b, t, d = x.shape
    t_ctx = ctx.shape[1]
    depth = mod_w.shape[0]
    n_mix = 3
    n_heads = d // HEAD_DIM
    assert 2 * N_DIRS * n_heads <= LANES
    gdn_layers = [i for i in range(depth) if i % n_mix == 2]
    last_ctx = gdn_layers[-1] if gdn_layers else -1

    lat = _Stream(b, t, min(ROW_TILE, t))
    cst = _Stream(b, t_ctx, min(ROW_TILE, t_ctx), ctx_row=b)
    lat.proj = _Stream(b, t, min(IN_ROW_TILE, t))
    cst.proj = _Stream(b, t_ctx, min(IN_ROW_TILE, t_ctx), ctx_row=b)
    modr = _mod_vectors(c, c_ctx, mod_w, mod_b)

    y = x.reshape(b * t, d)
    yc = ctx.reshape(b * t_ctx, d)

    def h_dtype(i):
        return F32 if i % n_mix == 0 else BF16

    h = _norm_mod(y, modr, 0, norm1_g[0], lat, h_dtype(0))
    hc = _norm_mod(yc, modr, 0, norm1_g[0], cst, h_dtype(0)) if last_ctx >= 0 else None

    ffn_w_in, ffn_w_out, pool_w, sc_w_in, sc_w_out, gdn_w_in, gdn_w_out = (
        w.astype(BF16) for w in (ffn_w_in, ffn_w_out, pool_w, sc_w_in, sc_w_out, gdn_w_in, gdn_w_out))

    out = None
    for i in range(depth):
        kind, jx = i % n_mix, i // n_mix
        ctx_stream = i < last_ctx
        last = i == depth - 1
        streams = [(lat, h, y, t, GRID_W)]
        if ctx_stream:
            streams.append((cst, hc, yc, t_ctx, None))

        if kind == 2:
            ab_par = jnp.pad(jnp.stack([gdn_a_log[jx].reshape(-1), gdn_dt_bias[jx].reshape(-1)]),
                             ((0, 0), (0, LANES - N_DIRS * n_heads)))
            qc, gbc, gbtc = _gdn_in(hc, gdn_w_in, gdn_conv, jx, ab_par, cst.proj, n_heads)
            s0 = jnp.zeros((b, n_heads, N_DIRS, HEAD_DIM, HEAD_DIM), F32)
            ocf, ocb, s_ctx = _gdn_core(qc, gbc, gbtc, s0, b, t_ctx, n_heads, BF16)
            ql, gbl, gbtl = _gdn_in(h, gdn_w_in, gdn_conv, jx, ab_par, lat.proj, n_heads)
            o_f, o_b, _ = _gdn_core(ql, gbl, gbtl, s_ctx, b, t, n_heads, BF16)
            mixed = [(_gdn_combine(o_f, o_b, ql, gdn_norm_g[jx], lat.tm), gdn_w_out, None)]
            if ctx_stream:
                mixed.append((_gdn_combine(ocf, ocb, qc, gdn_norm_g[jx], cst.tm), gdn_w_out, None))
        elif kind == 1:
            mixed = [(_sc_in(hs, sc_w_in, sc_conv, jx, st), sc_w_out, None) for st, hs, _, _, _ in streams]
        else:
            mixed = [(_pool_delta(hs, b, sl, gw), pool_w, pool_scale[jx]) for st, hs, _, sl, gw in streams]

        new = []
        for (st, _, ys, _, _), (m_lhs, w_out, scale) in zip(streams, mixed):
            y1, h2 = _mm_out(m_lhs, w_out, jx, ys, modr, i, 2, st, norm2_g[i], (i, 3, 4), BF16, col_scale=scale)
            f = _ffn_in(h2, ffn_w_in, ffn_conv, i, st.proj)
            if last:
                _, hn = _mm_out(f, ffn_w_out, i, y1, modr, i, 5, st, final_g, None, F32, emit_y=False)
                y2 = None
            else:
                y2, hn = _mm_out(f, ffn_w_out, i, y1, modr, i, 5, st, norm1_g[i + 1], (i + 1, 0, 1),
                                 h_dtype(i + 1))
            new.append((y2, hn))
        y, h = new[0]
        if ctx_stream:
            yc, hc = new[1]
        if last:
            out = h
    return out.reshape(b, t, d).astype(x.dtype)
```

```python
import functools
import math

import jax
import jax.numpy as jnp
from jax import lax
from jax.experimental import pallas as pl
from jax.experimental.pallas import tpu as pltpu

F32 = jnp.float32
BF16 = jnp.bfloat16

GRID_W = 64
POOL_WINDOWS = (2, 4, 8, 16)
HEAD_DIM = 128
N_DIRS = 2
NORM_EPS = 1e-6
SCAN_BLOCK = 128
HALO = 16
LANES = 128
VMEM_LIMIT_BYTES = 56 * 1024 * 1024
ROW_TILE = 512
IN_ROW_TILE = 1024
COL_TILE = 512
GDN_COL_TILE = 1024
WIDE_COL_TILE = 1024
WIDE_COL_MAX_K = 1024
DOT_ROWS = 128
POOL_COLS = 256
CORE_HEADS = 16


def _params(*sem):
    return pltpu.CompilerParams(dimension_semantics=sem, vmem_limit_bytes=VMEM_LIMIT_BYTES)


def _col_tile(n, cap=None):
    cap = COL_TILE if cap is None else cap
    for t in (cap, 512, 256, 128):
        if t <= cap and n % t == 0:
            return t
    raise ValueError(f"no column tile for {n}")


def _silu(x):
    return x * jax.nn.sigmoid(x)


def _rmsnorm(y, g):
    return y * lax.rsqrt(jnp.mean(y * y, axis=-1, keepdims=True) + NORM_EPS) * g


def _mod_kernel(c_ref, w_ref, b_ref, o_ref):
    s = _silu(c_ref[...]).astype(BF16)
    o_ref[...] = jnp.dot(s, w_ref[...].astype(BF16), preferred_element_type=F32) + b_ref[...]


def _mod_vectors(c, c_ctx, mod_w, mod_b):
    depth, d, d6 = mod_w.shape
    b = c.shape[0]
    r = -(-(b + 1) // 16) * 16
    cc = jnp.zeros((r, d), F32).at[:b].set(c).at[b].set(c_ctx)
    tn = _col_tile(d6)
    out = pl.pallas_call(
        _mod_kernel,
        grid=(depth, d6 // tn),
        in_specs=[pl.BlockSpec((r, d), lambda l, j: (0, 0)),
                  pl.BlockSpec((None, d, tn), lambda l, j: (l, 0, j)),
                  pl.BlockSpec((None, 1, tn), lambda l, j: (l, 0, j))],
        out_specs=pl.BlockSpec((None, r, tn), lambda l, j: (l, 0, j)),
        out_shape=jax.ShapeDtypeStruct((depth, r, d6), F32),
        compiler_params=_params("parallel", "parallel"),
        name="mod",
    )(cc, mod_w, mod_b.reshape(depth, 1, d6))
    return out.reshape(depth, r, 6, d)


class _Stream:
    def __init__(self, n_seq, seq_len, tm, ctx_row=None):
        assert seq_len % tm == 0 and tm % HALO == 0
        self.n_seq, self.seq_len, self.tm = n_seq, seq_len, tm
        self.tiles_per_seq = seq_len // tm
        self.rows = n_seq * seq_len
        self.n_tiles = self.rows // tm
        self.ctx_row = ctx_row

    def mod_row(self, i):
        return i // self.tiles_per_seq if self.ctx_row is None else self.ctx_row

    def mod_spec(self, layer, d, cols=None):
        if cols is None:
            return pl.BlockSpec((None, None, 6, d), lambda i, j: (layer, self.mod_row(i), 0, 0))
        return pl.BlockSpec((None, None, 6, cols), lambda i, j: (layer, self.mod_row(i), 0, j))


def _norm_kernel(y_ref, mod_ref, g_ref, o_ref):
    m = mod_ref[...]
    h = _rmsnorm(y_ref[...], g_ref[...]) * (1.0 + m[1:2]) + m[0:1]
    o_ref[...] = h.astype(o_ref.dtype)


def _norm_mod(y, modr, layer, g, st, out_dtype):
    rows, d = y.shape
    return pl.pallas_call(
        _norm_kernel,
        grid=(st.n_tiles, 1),
        in_specs=[pl.BlockSpec((st.tm, d), lambda i, j: (i, 0)),
                  st.mod_spec(layer, d),
                  pl.BlockSpec((1, d), lambda i, j: (0, 0))],
        out_specs=pl.BlockSpec((st.tm, d), lambda i, j: (i, 0)),
        out_shape=jax.ShapeDtypeStruct((rows, d), out_dtype),
        compiler_params=_params("parallel", "arbitrary"),
        name="norm_mod",
    )(y, modr, g.reshape(1, d))


def _window_sum(x, pos, n, unit, w):
    tot = x.shape[0]
    fwd, bwd = w - w // 2, w // 2
    assert fwd & (fwd - 1) == 0 and bwd & (bwd - 1) == 0 and bwd >= 1
    a, s = x, 1
    while s < fwd:
        a = a + jnp.where(pos + s < n, pltpu.roll(a, tot - s * unit, 0), 0.0)
        s *= 2
    b, s = jnp.where(pos >= 1, pltpu.roll(x, unit, 0), 0.0), 1
    while s < bwd:
        b = b + jnp.where(pos - s >= 0, pltpu.roll(b, s * unit, 0), 0.0)
        s *= 2
    return a + b


def _window_sum_strided(x, unit, w):
    assert w & (w - 1) == 0 and unit % 8 == 0
    pad = jnp.zeros(((w // 2) * unit, x.shape[1]), x.dtype)
    s, k = jnp.concatenate([pad, x, pad], axis=0), 1
    while k < w:
        m = s.shape[0] - k * unit
        s = s[0:m] + s[k * unit:k * unit + m]
        k *= 2
    return s[0:x.shape[0]]


def _window_count(pos, n, w):
    return (jnp.minimum(pos + (w - w // 2), n) - jnp.maximum(pos - w // 2, 0)).astype(F32)


def _pool_kernel(h_ref, o_ref, *, tiles_per_group, grid_w, grid_rows):
    group = pl.program_id(1) // tiles_per_group
    t = h_ref.shape[0]
    tok = lax.broadcasted_iota(jnp.int32, (t, 1), 0)
    for gi, w in enumerate(POOL_WINDOWS):
        @pl.when(group == gi)
        def _(w=w):
            x = h_ref[...]
            if grid_rows is None:
                s = _window_sum(x, tok, t, 1, w)
                cnt = _window_count(tok, t, w)
            else:
                shift = grid_w.bit_length() - 1
                col, row = tok & (grid_w - 1), tok >> shift
                s = _window_sum_strided(x, grid_w, w)
                s = _window_sum(s, col, grid_w, 1, w)
                cnt = _window_count(row, grid_rows, w) * _window_count(col, grid_w, w)
            o_ref[...] = (s / cnt - x).astype(o_ref.dtype)


def _pool_delta(h, n_seq, seq_len, grid_w):
    rows, d = h.shape
    group = d // len(POOL_WINDOWS)
    ct = min(POOL_COLS, group)
    assert group % ct == 0
    if grid_w is not None:
        assert grid_w & (grid_w - 1) == 0 and seq_len % grid_w == 0
    kern = functools.partial(_pool_kernel, tiles_per_group=group // ct, grid_w=grid_w,
                             grid_rows=None if grid_w is None else seq_len // grid_w)
    return pl.pallas_call(
        kern,
        grid=(n_seq, d // ct),
        in_specs=[pl.BlockSpec((seq_len, ct), lambda b, j: (b, j))],
        out_specs=pl.BlockSpec((seq_len, ct), lambda b, j: (b, j)),
        out_shape=jax.ShapeDtypeStruct((rows, d), BF16),
        compiler_params=_params("parallel", "parallel"),
        name="pool",
    )(h)


def _halo_specs(st, d):
    per = st.tm // HALO
    last = st.rows // HALO - 1
    return [pl.BlockSpec((st.tm, d), lambda i, j: (i, 0)),
            pl.BlockSpec((HALO, d), lambda i, j: (jnp.maximum(i * per - 1, 0), 0)),
            pl.BlockSpec((HALO, d), lambda i, j: (jnp.minimum((i + 1) * per, last), 0))]


def _w_spec(layer, d, tn, offset=0):
    return pl.BlockSpec((None, d, tn), lambda i, j: (layer, 0, j + offset))


def _fill_lhs(lhs_ref, hm_ref, hp_ref, hn_ref, tiles_per_seq):
    tm = hm_ref.shape[0]
    i = pl.program_id(0) % tiles_per_seq
    r = lax.broadcasted_iota(jnp.int32, hp_ref.shape, 0)
    nxt = jnp.where(i == tiles_per_seq - 1, jnp.zeros_like(hn_ref[...]), hn_ref[...])
    prv = jnp.where(i == 0, jnp.zeros_like(hp_ref[...]), hp_ref[...])
    lhs_ref[0:HALO, :] = jnp.where(r < HALO // 2, nxt, prv)
    lhs_ref[HALO:HALO + tm, :] = hm_ref[...]


def _dot_row_chunks(lhs_ref, w, first=0):
    n = lhs_ref.shape[0]
    bounds = [first] + list(range(first + (n - first) % DOT_ROWS + DOT_ROWS, n, DOT_ROWS)) + [n]
    parts = [jnp.dot(lhs_ref[r0:r1, :], w, preferred_element_type=F32) for r0, r1 in zip(bounds, bounds[1:])]
    return parts[0] if len(parts) == 1 else jnp.concatenate(parts, axis=0)


def _conv3(a_ext, cw, tm):
    n = a_ext.shape[0]
    dn = pltpu.roll(a_ext, 1, 0)[HALO:HALO + tm]
    up = pltpu.roll(a_ext, n - 1, 0)[HALO:HALO + tm]
    return cw[0:1] * dn + cw[1:2] * a_ext[HALO:HALO + tm] + cw[2:3] * up


def _ffn_in_kernel(hm_ref, hp_ref, hn_ref, wa_ref, wu_ref, cw_ref, o_ref, lhs_ref, *, tiles_per_seq):
    tm = hm_ref.shape[0]

    @pl.when(pl.program_id(1) == 0)
    def _():
        _fill_lhs(lhs_ref, hm_ref, hp_ref, hn_ref, tiles_per_seq)

    a = jnp.dot(lhs_ref[...], wa_ref[...], preferred_element_type=F32)
    u = jnp.dot(lhs_ref[HALO:HALO + tm, :], wu_ref[...], preferred_element_type=F32)
    o_ref[...] = (_silu(_conv3(a, cw_ref[...], tm)) * u).astype(o_ref.dtype)


def _ffn_in(h, w_in, conv, layer, st):
    rows, d = h.shape
    f = conv.shape[-1]
    tn = _col_tile(f)
    nj = f // tn
    kern = functools.partial(_ffn_in_kernel, tiles_per_seq=st.tiles_per_seq)
    return pl.pallas_call(
        kern,
        grid=(st.n_tiles, nj),
        in_specs=_halo_specs(st, d) + [_w_spec(layer, d, tn), _w_spec(layer, d, tn, nj),
                                       pl.BlockSpec((None, 3, tn), lambda i, j: (layer, 0, j))],
        out_specs=pl.BlockSpec((st.tm, tn), lambda i, j: (i, j)),
        out_shape=jax.ShapeDtypeStruct((rows, f), BF16),
        scratch_shapes=[pltpu.VMEM((st.tm + HALO, d), BF16)],
        compiler_params=_params("parallel", "arbitrary"),
        name="ffn_in",
    )(h, h, h, w_in, w_in, conv)


def _sc_in_kernel(hm_ref, hp_ref, hn_ref, wb_ref, wc_ref, wv_ref, cw_ref, o_ref, lhs_ref, *, tiles_per_seq):
    tm = hm_ref.shape[0]

    @pl.when(pl.program_id(1) == 0)
    def _():
        _fill_lhs(lhs_ref, hm_ref, hp_ref, hn_ref, tiles_per_seq)

    lhs = lhs_ref[...]
    cg = jnp.dot(lhs, wc_ref[...], preferred_element_type=F32)
    v = jnp.dot(lhs, wv_ref[...], preferred_element_type=F32)
    bg = jnp.dot(lhs_ref[HALO:HALO + tm, :], wb_ref[...], preferred_element_type=F32)
    o_ref[...] = (bg * _conv3(cg * v, cw_ref[...], tm)).astype(o_ref.dtype)


def _sc_in(h, w_in, conv, layer, st):
    rows, d = h.shape
    n = conv.shape[-1]
    tn = _col_tile(n)
    nj = n // tn
    kern = functools.partial(_sc_in_kernel, tiles_per_seq=st.tiles_per_seq)
    return pl.pallas_call(
        kern,
        grid=(st.n_tiles, nj),
        in_specs=_halo_specs(st, d) + [_w_spec(layer, d, tn), _w_spec(layer, d, tn, nj),
                                       _w_spec(layer, d, tn, 2 * nj),
                                       pl.BlockSpec((None, 3, tn), lambda i, j: (layer, 0, j))],
        out_specs=pl.BlockSpec((st.tm, tn), lambda i, j: (i, j)),
        out_shape=jax.ShapeDtypeStruct((rows, n), BF16),
        scratch_shapes=[pltpu.VMEM((st.tm + HALO, d), BF16)],
        compiler_params=_params("parallel", "arbitrary"),
        name="sc_in",
    )(h, h, h, w_in, w_in, w_in, conv)


def _block_scan(x, pos, block, reverse):
    n = x.shape[0]
    s = 1
    while s < block:
        if reverse:
            x = x + jnp.where(pos + s < block, pltpu.roll(x, n - s, 0), 0.0)
        else:
            x = x + jnp.where(pos >= s, pltpu.roll(x, s, 0), 0.0)
        s *= 2
    return x


def _gdn_in_kernel(hm_ref, hp_ref, hn_ref, w_ref, wab_ref, cw_ref, ab_ref, o_ref, gb_ref, gbt_ref, lhs_ref,
                   *, tiles_per_seq, n_qk_tiles, n_conv_tiles, n_heads):
    tm = hm_ref.shape[0]
    tn = w_ref.shape[1]
    j = pl.program_id(1)

    @pl.when(j == 0)
    def _():
        _fill_lhs(lhs_ref, hm_ref, hp_ref, hn_ref, tiles_per_seq)
        lane = lax.broadcasted_iota(jnp.int32, (1, LANES), 1)
        ab = jnp.dot(lhs_ref[HALO:HALO + tm, :], wab_ref[...], preferred_element_type=F32)
        ab = jnp.where(lane < 2 * N_DIRS * n_heads, ab, 0.0)
        p = ab_ref[...]
        x = ab + p[1:2]
        log_alpha = -jnp.exp(p[0:1]) * (jnp.maximum(x, 0.0) + jnp.log1p(jnp.exp(-jnp.abs(x))))
        beta = jax.nn.sigmoid(ab)
        pos = lax.broadcasted_iota(jnp.int32, (tm, 1), 0) & (SCAN_BLOCK - 1)
        g = jnp.where(lane < n_heads, _block_scan(log_alpha, pos, SCAN_BLOCK, False),
                      _block_scan(log_alpha, pos, SCAN_BLOCK, True))
        gb = jnp.where(lane < N_DIRS * n_heads, g, beta)
        gb_ref[...] = gb
        gbt_ref[...] = gb.T

    scale = jnp.where(j < n_qk_tiles // 2, HEAD_DIM ** -0.5, 1.0).astype(F32)
    sub = min(tn, COL_TILE)
    starts = range(0, tn, sub)
    parts = [_dot_row_chunks(lhs_ref, w_ref[:, c0:c0 + sub]) for c0 in starts]
    for c0, a in zip(starts, parts):
        x = jnp.where(j < n_conv_tiles, _silu(_conv3(a, cw_ref[:, c0:c0 + sub], tm)), a[HALO:HALO + tm])
        for hh in range(sub // HEAD_DIM):
            xh = x[:, hh * HEAD_DIM:(hh + 1) * HEAD_DIM]
            inv = lax.rsqrt(jnp.sum(xh * xh, axis=-1, keepdims=True) + NORM_EPS) * scale
            o_ref[c0 // HEAD_DIM + hh] = (xh * jnp.where(j < n_qk_tiles, inv, 1.0)).astype(o_ref.dtype)


def _gdn_in(h, w_in, conv, layer, ab_par, st, n_heads):
    rows, d = h.shape
    n = 4 * d
    assert w_in.shape[-1] - n <= LANES
    tn = _col_tile(d, GDN_COL_TILE)
    nj = n // tn
    n_conv = conv.shape[-1] // tn
    kern = functools.partial(_gdn_in_kernel, tiles_per_seq=st.tiles_per_seq, n_qk_tiles=2 * d // tn,
                             n_conv_tiles=n_conv, n_heads=n_heads)
    hpt = tn // HEAD_DIM
    return pl.pallas_call(
        kern,
        grid=(st.n_tiles, nj),
        in_specs=_halo_specs(st, d) + [
            _w_spec(layer, d, tn),
            pl.BlockSpec((None, d, LANES), lambda i, j: (layer, 0, n // LANES)),
            pl.BlockSpec((None, 3, tn), lambda i, j: (layer, 0, jnp.minimum(j, n_conv - 1))),
            pl.BlockSpec((2, LANES), lambda i, j: (0, 0))],
        out_specs=[pl.BlockSpec((hpt, st.tm, HEAD_DIM), lambda i, j: (j, i, 0)),
                   pl.BlockSpec((st.tm, LANES), lambda i, j: (i, 0)),
                   pl.BlockSpec((LANES, st.tm), lambda i, j: (0, i))],
        out_shape=[jax.ShapeDtypeStruct((n // HEAD_DIM, rows, HEAD_DIM), BF16),
                   jax.ShapeDtypeStruct((rows, LANES), F32),
                   jax.ShapeDtypeStruct((LANES, rows), F32)],
        scratch_shapes=[pltpu.VMEM((st.tm + HALO, d), BF16)],
        compiler_params=_params("parallel", "arbitrary"),
        name="gdn_in",
    )(h, h, h, w_in, w_in, conv, ab_par)


def _each(f, *lists):
    return [f(*xs) for xs in zip(*lists)]


def _bdot(a, b):
    return _each(lambda x, y: jnp.dot(x.astype(BF16), y.astype(BF16), preferred_element_type=F32), a, b)


def _unit_tri_solve(neg_l, rhs, ri, ci, upper):
    c = neg_l[0].shape[0]
    base = 8

    def same_block(width):
        sh = width.bit_length() - 1
        return (ri >> sh) == (ci >> sh)

    diag = same_block(base)
    n0 = [jnp.where(diag, m, 0.0) for m in neg_l]
    p2 = _bdot(n0, n0)
    t = _bdot([jnp.concatenate([a, b], axis=0) for a, b in zip(n0, p2)], p2)
    x = _each(lambda a, b, tt: a + b + tt[0:c], n0, p2, t)
    p4 = [tt[c:2 * c] for tt in t]
    x = _each(lambda xx, p, u: xx + p + u, x, p4, _bdot(x, p4))
    size = base
    while size < c:
        pair = jnp.logical_and(same_block(2 * size), jnp.logical_not(same_block(size)))
        cm = [jnp.where(pair, m, 0.0) for m in neg_l]
        blocks = [range(0 if up else 1, c // size, 2) for up in upper]
        pick = lambda m, bl: jnp.concatenate([m[b * size:(b + 1) * size] for b in bl], axis=0)
        cm_h = _each(pick, cm, blocks)
        a_h = _each(lambda m, u: m + u, cm_h, _bdot(_each(pick, x, blocks), cm))
        a_h = _each(lambda aa, u: aa + u, a_h, _bdot(a_h, x))
        zero = jnp.zeros((size, c), F32)

        def spread(aa, bl):
            slabs = [aa[k * size:(k + 1) * size] for k in range(len(bl))]
            return jnp.concatenate([s for k in range(len(bl)) for s in
                                    ((slabs[k], zero) if bl[0] == 0 else (zero, slabs[k]))], axis=0)

        x = _each(lambda xx, aa, bl: xx + spread(aa, bl), x, a_h, blocks)
        size *= 2
    return _each(lambda r, u: r + u, rhs, _bdot(x, rhs))


def _delta_blocks(q, k, v, gcol, bcol, grow, s, reverse):
    c = q[0].shape[0]
    ri = lax.broadcasted_iota(jnp.int32, (c, c), 0)
    ci = lax.broadcasted_iota(jnp.int32, (c, c), 1)
    seen = [(ri <= ci) if r else (ri >= ci) for r in reverse]
    before = [(ri < ci) if r else (ri > ci) for r in reverse]
    g_end = [g[0:1] if r else g[c - 1:c] for g, r in zip(gcol, reverse)]
    kf = [x.astype(F32) for x in k]
    kb = _each(lambda x, b: x * b, kf, bcol)
    eg = [jnp.exp(g) for g in gcol]
    kq = _each(lambda x, y, z: lax.dot_general(jnp.concatenate([x.astype(BF16), y], axis=0), z,
                                               (((1,), (1,)), ((), ())), preferred_element_type=F32), kb, q, k)
    gam = _each(lambda m, gc, gr: jnp.exp(jnp.where(m, gc - gr, -jnp.inf)), seen, gcol, grow)
    neg_l = _each(lambda m, x, gm: jnp.where(m, -(x[0:c] * gm), 0.0), before, kq, gam)
    qk = _each(lambda x, gm: x[c:2 * c] * gm, kq, gam)
    rhs = _each(lambda x, b, y, e: jnp.concatenate([x.astype(F32) * b, y * e], axis=1), v, bcol, kb, eg)
    y = _unit_tri_solve(neg_l, rhs, ri, ci, reverse)
    q_dec = _each(lambda x, e: x.astype(F32) * e, q, eg)
    k_dec = _each(lambda x, ge, g: x * jnp.exp(ge - g), kf, g_end, gcol)
    vs = _bdot(_each(lambda yy, qd: jnp.concatenate([yy[:, HEAD_DIM:2 * HEAD_DIM], qd], axis=0), y, q_dec), s)
    v_new = _each(lambda yy, u: (yy[:, 0:HEAD_DIM] - u[0:c]).astype(BF16), y, vs)
    o = _each(lambda u, w: u[c:2 * c] + w, vs, _bdot(qk, v_new))
    kv = _each(lambda x, w: lax.dot_general(x.astype(BF16), w, (((0,), (0,)), ((), ())),
                                            preferred_element_type=F32), k_dec, v_new)
    s_new = _each(lambda ss, ge, u: ss * jnp.exp(ge) + u, s, g_end, kv)
    return o, s_new


def _gdn_core_kernel(qf_ref, kf_ref, vf_ref, qb_ref, kb_ref, vb_ref, gbf_ref, gbb_ref, gtf_ref, gtb_ref,
                     s0_ref, of_ref, ob_ref, sout_ref, s_ref, *, n_heads):
    n = pl.program_id(2)
    heads = qf_ref.shape[0]
    h0 = pl.program_id(1) * heads

    @pl.when(n == 0)
    def _():
        s_ref[...] = s0_ref[...]

    lane = lax.broadcasted_iota(jnp.int32, (1, LANES), 1)
    dirs = ((qf_ref, kf_ref, vf_ref, gbf_ref, gtf_ref, of_ref, False),
            (qb_ref, kb_ref, vb_ref, gbb_ref, gtb_ref, ob_ref, True))
    q, k, v, gcol, bcol, grow, s, reverse, dest = [], [], [], [], [], [], [], [], []
    for d, (q_ref, k_ref, v_ref, gb_ref, gt_ref, o_ref, rev) in enumerate(dirs):
        gb = gb_ref[...]
        for hh in range(heads):
            col = d * n_heads + h0 + hh
            gcol.append(jnp.sum(jnp.where(lane == col, gb, 0.0), axis=1, keepdims=True))
            bcol.append(jnp.sum(jnp.where(lane == col + N_DIRS * n_heads, gb, 0.0), axis=1, keepdims=True))
            grow.append(gt_ref[pl.ds(col, 1), :])
            q.append(q_ref[hh])
            k.append(k_ref[hh])
            v.append(v_ref[hh])
            s.append(s_ref[hh, d])
            reverse.append(rev)
            dest.append((o_ref, hh, d))
    o, s_new = _delta_blocks(q, k, v, gcol, bcol, grow, s, reverse)
    for (o_ref, hh, d), oo, ss in zip(dest, o, s_new):
        o_ref[hh] = oo.astype(o_ref.dtype)
        s_ref[hh, d] = ss

    @pl.when(n == pl.num_programs(2) - 1)
    def _():
        sout_ref[...] = s_ref[...]


def _gdn_core(qkvg, gb, gbt, s0, n_seq, seq_len, n_heads, o_dtype):
    rows = n_seq * seq_len
    c = SCAN_BLOCK
    assert seq_len % c == 0
    nc = seq_len // c
    g = min(CORE_HEADS, n_heads)
    assert n_heads % g == 0
    ng = n_heads // g

    def fwd(b, n):
        return b * nc + n

    def bwd(b, n):
        return b * nc + nc - 1 - n

    def head_spec(part, blk):
        return pl.BlockSpec((g, c, HEAD_DIM), lambda b, hg, n: (part * ng + hg, blk(b, n), 0))

    state_spec = pl.BlockSpec((None, g, N_DIRS, HEAD_DIM, HEAD_DIM), lambda b, hg, n: (b, hg, 0, 0, 0))
    kern = functools.partial(_gdn_core_kernel, n_heads=n_heads)
    return pl.pallas_call(
        kern,
        grid=(n_seq, ng, nc),
        in_specs=[head_spec(0, fwd), head_spec(1, fwd), head_spec(2, fwd),
                  head_spec(0, bwd), head_spec(1, bwd), head_spec(2, bwd),
                  pl.BlockSpec((c, LANES), lambda b, hg, n: (fwd(b, n), 0)),
                  pl.BlockSpec((c, LANES), lambda b, hg, n: (bwd(b, n), 0)),
                  pl.BlockSpec((LANES, c), lambda b, hg, n: (0, fwd(b, n))),
                  pl.BlockSpec((LANES, c), lambda b, hg, n: (0, bwd(b, n))),
                  state_spec],
        out_specs=[pl.BlockSpec((g, c, HEAD_DIM), lambda b, hg, n: (hg, fwd(b, n), 0)),
                   pl.BlockSpec((g, c, HEAD_DIM), lambda b, hg, n: (hg, bwd(b, n), 0)),
                   state_spec],
        out_shape=[jax.ShapeDtypeStruct((n_heads, rows, HEAD_DIM), o_dtype),
                   jax.ShapeDtypeStruct((n_heads, rows, HEAD_DIM), o_dtype),
                   jax.ShapeDtypeStruct(s0.shape, F32)],
        scratch_shapes=[pltpu.VMEM((g, N_DIRS, HEAD_DIM, HEAD_DIM), F32)],
        compiler_params=_params("parallel", "parallel", "arbitrary"),
        name="gdn_core",
    )(qkvg, qkvg, qkvg, qkvg, qkvg, qkvg, gb, gb, gbt, gbt, s0)


def _gdn_combine_kernel(of_ref, ob_ref, gate_ref, g_ref, o_ref):
    for hh in range(of_ref.shape[0]):
        o = of_ref[hh].astype(F32) + ob_ref[hh].astype(F32)
        on = _rmsnorm(o, g_ref[...])
        o_ref[:, hh * HEAD_DIM:(hh + 1) * HEAD_DIM] = (on * _silu(gate_ref[hh].astype(F32))).astype(o_ref.dtype)


def _gdn_combine(o_f, o_b, qkvg, norm_g, tm):
    n_heads, rows, _ = o_f.shape
    g = min(CORE_HEADS, n_heads)
    ng = n_heads // g
    return pl.pallas_call(
        _gdn_combine_kernel,
        grid=(rows // tm, ng),
        in_specs=[pl.BlockSpec((g, tm, HEAD_DIM), lambda i, j: (j, i, 0)),
                  pl.BlockSpec((g, tm, HEAD_DIM), lambda i, j: (j, i, 0)),
                  pl.BlockSpec((g, tm, HEAD_DIM), lambda i, j: (3 * ng + j, i, 0)),
                  pl.BlockSpec((1, HEAD_DIM), lambda i, j: (0, 0))],
        out_specs=pl.BlockSpec((tm, g * HEAD_DIM), lambda i, j: (i, j)),
        out_shape=jax.ShapeDtypeStruct((rows, n_heads * HEAD_DIM), BF16),
        compiler_params=_params("parallel", "parallel"),
        name="gdn_combine",
    )(o_f, o_b, qkvg, norm_g.reshape(1, HEAD_DIM))


def _mm_out_kernel(*refs, gate_row, next_rows, has_scale, emit_y):
    it = iter(refs)
    lhs_ref, w_ref, y_ref, gate_ref = next(it), next(it), next(it), next(it)
    scale_ref = next(it) if has_scale else None
    gn_ref = next(it)
    modn_ref = next(it) if next_rows is not None else None
    ynew_ref = next(it) if emit_y else None
    hn_ref, row_ref = next(it), next(it)
    j = pl.program_id(1)
    nj, _, tn = row_ref.shape

    acc = jnp.dot(lhs_ref[...], w_ref[...], preferred_element_type=F32)
    if has_scale:
        acc = acc * scale_ref[...]
    y_new = y_ref[...] + gate_ref[gate_row:gate_row + 1, :] * acc
    if emit_y:
        ynew_ref[...] = y_new
    row_ref[j] = y_new

    @pl.when(j == nj - 1)
    def _():
        ss = jnp.zeros((row_ref.shape[1], 1), F32)
        for jj in range(nj):
            yj = row_ref[jj]
            ss = ss + jnp.sum(yj * yj, axis=-1, keepdims=True)
        inv = lax.rsqrt(ss / (nj * tn) + NORM_EPS)
        for jj in range(nj):
            cols = slice(jj * tn, (jj + 1) * tn)
            hj = row_ref[jj] * inv * gn_ref[:, cols]
            if next_rows is not None:
                sh, sc = next_rows
                hj = hj * (1.0 + modn_ref[sc:sc + 1, cols]) + modn_ref[sh:sh + 1, cols]
            hn_ref[:, cols] = hj.astype(hn_ref.dtype)


def _mm_out(lhs, w, w_layer, y, modr, layer, gate_row, st, next_g, next_mod, out_dtype, col_scale=None,
            emit_y=True):
    rows, n = y.shape
    grouped = w.ndim == 4
    k = w.shape[-2]
    group_n = w.shape[-1]
    tn = _col_tile(group_n, WIDE_COL_TILE if k <= WIDE_COL_MAX_K else None)
    nj = n // tn
    per_group = group_n // tn
    if grouped:
        lhs_spec = pl.BlockSpec((st.tm, k), lambda i, j: (i, j // per_group))
        w_spec = pl.BlockSpec((None, None, k, tn), lambda i, j: (w_layer, j // per_group, 0, j % per_group))
    else:
        lhs_spec = pl.BlockSpec((st.tm, k), lambda i, j: (i, 0))
        w_spec = pl.BlockSpec((None, k, tn), lambda i, j: (w_layer, 0, j))
    in_specs = [lhs_spec, w_spec,
                pl.BlockSpec((st.tm, tn), lambda i, j: (i, j)),
                st.mod_spec(layer, n, tn)]
    args = [lhs, w, y, modr]
    if col_scale is not None:
        in_specs.append(pl.BlockSpec((1, tn), lambda i, j: (0, j)))
        args.append(col_scale.reshape(1, n))
    in_specs.append(pl.BlockSpec((1, n), lambda i, j: (0, 0)))
    args.append(next_g.reshape(1, n))
    next_rows = None
    if next_mod is not None:
        in_specs.append(st.mod_spec(next_mod[0], n))
        args.append(modr)
        next_rows = (next_mod[1], next_mod[2])
    out_specs, out_shape = [], []
    if emit_y:
        out_specs.append(pl.BlockSpec((st.tm, tn), lambda i, j: (i, j)))
        out_shape.append(jax.ShapeDtypeStruct((rows, n), F32))
    out_specs.append(pl.BlockSpec((st.tm, n), lambda i, j: (i, 0)))
    out_shape.append(jax.ShapeDtypeStruct((rows, n), out_dtype))
    kern = functools.partial(_mm_out_kernel, gate_row=gate_row, next_rows=next_rows,
                             has_scale=col_scale is not None, emit_y=emit_y)
    res = pl.pallas_call(
        kern,
        grid=(st.n_tiles, nj),
        in_specs=in_specs,
        out_specs=out_specs,
        out_shape=out_shape,
        scratch_shapes=[pltpu.VMEM((nj, st.tm, tn), F32)],
        compiler_params=_params("parallel", "arbitrary"),
        name="mm_out",
    )(*args)
    return res if emit_y else (None, res[0])


def kernel(x, c, ctx, c_ctx, mod_w, mod_b, norm1_g, norm2_g, ffn_w_in, ffn_conv, ffn_w_out, pool_w, pool_scale, sc_w_in, sc_conv, sc_w_out, gdn_w_in, gdn_conv, gdn_a_log, gdn_dt_bias, gdn_norm_g, gdn_w_out, final_g):
    b, t, d = x.shape
    t_ctx = ctx.shape[1]
    depth = mod_w.shape[0]
    n_mix = 3
    n_heads = d // HEAD_DIM
    assert 2 * N_DIRS * n_heads <= LANES
    gdn_layers = [i for i in range(depth) if i % n_mix == 2]
    last_ctx = gdn_layers[-1] if gdn_layers else -1

    lat = _Stream(b, t, min(ROW_TILE, t))
    cst = _Stream(b, t_ctx, min(ROW_TILE, t_ctx), ctx_row=b)
    lat.proj = _Stream(b, t, min(IN_ROW_TILE, t))
    cst.proj = _Stream(b, t_ctx, min(IN_ROW_TILE, t_ctx), ctx_row=b)
    modr = _mod_vectors(c, c_ctx, mod_w, mod_b)

    y = x.reshape(b * t, d)
    yc = ctx.reshape(b * t_ctx, d)

    def h_dtype(i):
        return F32 if i % n_mix == 0 else BF16

    h = _norm_mod(y, modr, 0, norm1_g[0], lat, h_dtype(0))
    hc = _norm_mod(yc, modr, 0, norm1_g[0], cst, h_dtype(0)) if last_ctx >= 0 else None

    ffn_w_in, ffn_w_out, pool_w, sc_w_in, sc_w_out, gdn_w_in, gdn_w_out = (
        w.astype(BF16) for w in (ffn_w_in, ffn_w_out, pool_w, sc_w_in, sc_w_out, gdn_w_in, gdn_w_out))

    out = None
    for i in range(depth):
        kind, jx = i % n_mix, i // n_mix
        ctx_stream = i < last_ctx
        last = i == depth - 1
        streams = [(lat, h, y, t, GRID_W)]
        if ctx_stream:
            streams.append((cst, hc, yc, t_ctx, None))

        if kind == 2:
            ab_par = jnp.pad(jnp.stack([gdn_a_log[jx].reshape(-1), gdn_dt_bias[jx].reshape(-1)]),
                             ((0, 0), (0, LANES - N_DIRS * n_heads)))
            qc, gbc, gbtc = _gdn_in(hc, gdn_w_in, gdn_conv, jx, ab_par, cst, n_heads)
            s0 = jnp.zeros((b, n_heads, N_DIRS, HEAD_DIM, HEAD_DIM), F32)
            ocf, ocb, s_ctx = _gdn_core(qc, gbc, gbtc, s0, b, t_ctx, n_heads, BF16)
            ql, gbl, gbtl = _gdn_in(h, gdn_w_in, gdn_conv, jx, ab_par, lat, n_heads)
            o_f, o_b, _ = _gdn_core(ql, gbl, gbtl, s_ctx, b, t, n_heads, BF16)
            mixed = [(_gdn_combine(o_f, o_b, ql, gdn_norm_g[jx], lat.tm), gdn_w_out, None)]
            if ctx_stream:
                mixed.append((_gdn_combine(ocf, ocb, qc, gdn_norm_g[jx], cst.tm), gdn_w_out, None))
        elif kind == 1:
            mixed = [(_sc_in(hs, sc_w_in, sc_conv, jx, st), sc_w_out, None) for st, hs, _, _, _ in streams]
        else:
            mixed = [(_pool_delta(hs, b, sl, gw), pool_w, pool_scale[jx]) for st, hs, _, sl, gw in streams]

        new = []
        for (st, _, ys, _, _), (m_lhs, w_out, scale) in zip(streams, mixed):
            y1, h2 = _mm_out(m_lhs, w_out, jx, ys, modr, i, 2, st, norm2_g[i], (i, 3, 4), BF16, col_scale=scale)
            f = _ffn_in(h2, ffn_w_in, ffn_conv, i, st.proj)
            if last:
                _, hn = _mm_out(f, ffn_w_out, i, y1, modr, i, 5, st, final_g, None, F32, emit_y=False)
                y2 = None
            else:
                y2, hn = _mm_out(f, ffn_w_out, i, y1, modr, i, 5, st, norm1_g[i + 1], (i + 1, 0, 1),
                                 h_dtype(i + 1))
            new.append((y2, hn))
        y, h = new[0]
        if ctx_stream:
            yc, hc = new[1]
        if last:
            out = h
    return out.reshape(b, t, d).astype(x.dtype)
```

```python
import functools
import math

import jax
import jax.numpy as jnp
from jax import lax
from jax.experimental import pallas as pl
from jax.experimental.pallas import tpu as pltpu

F32 = jnp.float32
BF16 = jnp.bfloat16

GRID_W = 64
POOL_WINDOWS = (2, 4, 8, 16)
HEAD_DIM = 128
N_DIRS = 2
NORM_EPS = 1e-6
SCAN_BLOCK = 128
HALO = 16
LANES = 128
VMEM_LIMIT_BYTES = 56 * 1024 * 1024
ROW_TILE = 512
IN_ROW_TILE = 1024
COL_TILE = 512
GDN_COL_TILE = 1024
WIDE_COL_TILE = 1024
DOT_ROWS = 128
POOL_COLS = 256
CORE_HEADS = 16


def _params(*sem):
    return pltpu.CompilerParams(dimension_semantics=sem, vmem_limit_bytes=VMEM_LIMIT_BYTES)


def _col_tile(n, cap=None):
    cap = COL_TILE if cap is None else cap
    for t in (cap, 512, 256, 128):
        if t <= cap and n % t == 0:
            return t
    raise ValueError(f"no column tile for {n}")


def _silu(x):
    return x * jax.nn.sigmoid(x)


def _rmsnorm(y, g):
    return y * lax.rsqrt(jnp.mean(y * y, axis=-1, keepdims=True) + NORM_EPS) * g


def _mod_kernel(c_ref, w_ref, b_ref, o_ref):
    s = _silu(c_ref[...]).astype(BF16)
    o_ref[...] = jnp.dot(s, w_ref[...].astype(BF16), preferred_element_type=F32) + b_ref[...]


def _mod_vectors(c, c_ctx, mod_w, mod_b):
    depth, d, d6 = mod_w.shape
    b = c.shape[0]
    r = -(-(b + 1) // 16) * 16
    cc = jnp.zeros((r, d), F32).at[:b].set(c).at[b].set(c_ctx)
    tn = _col_tile(d6)
    out = pl.pallas_call(
        _mod_kernel,
        grid=(depth, d6 // tn),
        in_specs=[pl.BlockSpec((r, d), lambda l, j: (0, 0)),
                  pl.BlockSpec((None, d, tn), lambda l, j: (l, 0, j)),
                  pl.BlockSpec((None, 1, tn), lambda l, j: (l, 0, j))],
        out_specs=pl.BlockSpec((None, r, tn), lambda l, j: (l, 0, j)),
        out_shape=jax.ShapeDtypeStruct((depth, r, d6), F32),
        compiler_params=_params("parallel", "parallel"),
        name="mod",
    )(cc, mod_w, mod_b.reshape(depth, 1, d6))
    return out.reshape(depth, r, 6, d)


class _Stream:
    def __init__(self, n_seq, seq_len, tm, ctx_row=None):
        assert seq_len % tm == 0 and tm % HALO == 0
        self.n_seq, self.seq_len, self.tm = n_seq, seq_len, tm
        self.tiles_per_seq = seq_len // tm
        self.rows = n_seq * seq_len
        self.n_tiles = self.rows // tm
        self.ctx_row = ctx_row

    def mod_row(self, i):
        return i // self.tiles_per_seq if self.ctx_row is None else self.ctx_row

    def mod_spec(self, layer, d, cols=None):
        if cols is None:
            return pl.BlockSpec((None, None, 6, d), lambda i, j: (layer, self.mod_row(i), 0, 0))
        return pl.BlockSpec((None, None, 6, cols), lambda i, j: (layer, self.mod_row(i), 0, j))


def _norm_kernel(y_ref, mod_ref, g_ref, o_ref):
    m = mod_ref[...]
    h = _rmsnorm(y_ref[...], g_ref[...]) * (1.0 + m[1:2]) + m[0:1]
    o_ref[...] = h.astype(o_ref.dtype)


def _norm_mod(y, modr, layer, g, st, out_dtype):
    rows, d = y.shape
    return pl.pallas_call(
        _norm_kernel,
        grid=(st.n_tiles, 1),
        in_specs=[pl.BlockSpec((st.tm, d), lambda i, j: (i, 0)),
                  st.mod_spec(layer, d),
                  pl.BlockSpec((1, d), lambda i, j: (0, 0))],
        out_specs=pl.BlockSpec((st.tm, d), lambda i, j: (i, 0)),
        out_shape=jax.ShapeDtypeStruct((rows, d), out_dtype),
        compiler_params=_params("parallel", "arbitrary"),
        name="norm_mod",
    )(y, modr, g.reshape(1, d))


def _window_sum(x, pos, n, unit, w):
    tot = x.shape[0]
    fwd, bwd = w - w // 2, w // 2
    assert fwd & (fwd - 1) == 0 and bwd & (bwd - 1) == 0 and bwd >= 1
    a, s = x, 1
    while s < fwd:
        a = a + jnp.where(pos + s < n, pltpu.roll(a, tot - s * unit, 0), 0.0)
        s *= 2
    b, s = jnp.where(pos >= 1, pltpu.roll(x, unit, 0), 0.0), 1
    while s < bwd:
        b = b + jnp.where(pos - s >= 0, pltpu.roll(b, s * unit, 0), 0.0)
        s *= 2
    return a + b


def _window_sum_strided(x, unit, w):
    assert w & (w - 1) == 0 and unit % 8 == 0
    pad = jnp.zeros(((w // 2) * unit, x.shape[1]), x.dtype)
    s, k = jnp.concatenate([pad, x, pad], axis=0), 1
    while k < w:
        m = s.shape[0] - k * unit
        s = s[0:m] + s[k * unit:k * unit + m]
        k *= 2
    return s[0:x.shape[0]]


def _window_count(pos, n, w):
    return (jnp.minimum(pos + (w - w // 2), n) - jnp.maximum(pos - w // 2, 0)).astype(F32)


def _pool_kernel(h_ref, o_ref, *, tiles_per_group, grid_w, grid_rows):
    group = pl.program_id(1) // tiles_per_group
    t = h_ref.shape[0]
    tok = lax.broadcasted_iota(jnp.int32, (t, 1), 0)
    for gi, w in enumerate(POOL_WINDOWS):
        @pl.when(group == gi)
        def _(w=w):
            x = h_ref[...]
            if grid_rows is None:
                s = _window_sum(x, tok, t, 1, w)
                cnt = _window_count(tok, t, w)
            else:
                shift = grid_w.bit_length() - 1
                col, row = tok & (grid_w - 1), tok >> shift
                s = _window_sum_strided(x, grid_w, w)
                s = _window_sum(s, col, grid_w, 1, w)
                cnt = _window_count(row, grid_rows, w) * _window_count(col, grid_w, w)
            o_ref[...] = (s / cnt - x).astype(o_ref.dtype)


def _pool_delta(h, n_seq, seq_len, grid_w):
    rows, d = h.shape
    group = d // len(POOL_WINDOWS)
    ct = min(POOL_COLS, group)
    assert group % ct == 0
    if grid_w is not None:
        assert grid_w & (grid_w - 1) == 0 and seq_len % grid_w == 0
    kern = functools.partial(_pool_kernel, tiles_per_group=group // ct, grid_w=grid_w,
                             grid_rows=None if grid_w is None else seq_len // grid_w)
    return pl.pallas_call(
        kern,
        grid=(n_seq, d // ct),
        in_specs=[pl.BlockSpec((seq_len, ct), lambda b, j: (b, j))],
        out_specs=pl.BlockSpec((seq_len, ct), lambda b, j: (b, j)),
        out_shape=jax.ShapeDtypeStruct((rows, d), BF16),
        compiler_params=_params("parallel", "parallel"),
        name="pool",
    )(h)


def _halo_specs(st, d):
    per = st.tm // HALO
    last = st.rows // HALO - 1
    return [pl.BlockSpec((st.tm, d), lambda i, j: (i, 0)),
            pl.BlockSpec((HALO, d), lambda i, j: (jnp.maximum(i * per - 1, 0), 0)),
            pl.BlockSpec((HALO, d), lambda i, j: (jnp.minimum((i + 1) * per, last), 0))]


def _col_blocked(w, tn):
    *lead, k, n = w.shape
    return jnp.swapaxes(w.reshape(*lead, k, n // tn, tn), -3, -2)


def _w_spec(layer, d, tn, offset=0):
    return pl.BlockSpec((None, None, d, tn), lambda i, j: (layer, j + offset, 0, 0))


def _fill_lhs(lhs_ref, hm_ref, hp_ref, hn_ref, tiles_per_seq):
    tm = hm_ref.shape[0]
    i = pl.program_id(0) % tiles_per_seq
    r = lax.broadcasted_iota(jnp.int32, hp_ref.shape, 0)
    nxt = jnp.where(i == tiles_per_seq - 1, jnp.zeros_like(hn_ref[...]), hn_ref[...])
    prv = jnp.where(i == 0, jnp.zeros_like(hp_ref[...]), hp_ref[...])
    lhs_ref[0:HALO, :] = jnp.where(r < HALO // 2, nxt, prv)
    lhs_ref[HALO:HALO + tm, :] = hm_ref[...]


def _dot_row_chunks(lhs_ref, w, first=0):
    n = lhs_ref.shape[0]
    bounds = [first] + list(range(first + (n - first) % DOT_ROWS + DOT_ROWS, n, DOT_ROWS)) + [n]
    parts = [jnp.dot(lhs_ref[r0:r1, :], w, preferred_element_type=F32) for r0, r1 in zip(bounds, bounds[1:])]
    return parts[0] if len(parts) == 1 else jnp.concatenate(parts, axis=0)


def _conv3(a_ext, cw, tm):
    n = a_ext.shape[0]
    dn = pltpu.roll(a_ext, 1, 0)[HALO:HALO + tm]
    up = pltpu.roll(a_ext, n - 1, 0)[HALO:HALO + tm]
    return cw[0:1] * dn + cw[1:2] * a_ext[HALO:HALO + tm] + cw[2:3] * up


def _ffn_in_kernel(hm_ref, hp_ref, hn_ref, wa_ref, wu_ref, cw_ref, o_ref, lhs_ref, *, tiles_per_seq):
    tm = hm_ref.shape[0]

    @pl.when(pl.program_id(1) == 0)
    def _():
        _fill_lhs(lhs_ref, hm_ref, hp_ref, hn_ref, tiles_per_seq)

    a = jnp.dot(lhs_ref[...], wa_ref[...], preferred_element_type=F32)
    u = jnp.dot(lhs_ref[HALO:HALO + tm, :], wu_ref[...], preferred_element_type=F32)
    o_ref[...] = (_silu(_conv3(a, cw_ref[...], tm)) * u).astype(o_ref.dtype)


def _ffn_in(h, w_in, conv, layer, st):
    rows, d = h.shape
    f = conv.shape[-1]
    tn = w_in.shape[-1]
    nj = f // tn
    kern = functools.partial(_ffn_in_kernel, tiles_per_seq=st.tiles_per_seq)
    return pl.pallas_call(
        kern,
        grid=(st.n_tiles, nj),
        in_specs=_halo_specs(st, d) + [_w_spec(layer, d, tn), _w_spec(layer, d, tn, nj),
                                       pl.BlockSpec((None, 3, tn), lambda i, j: (layer, 0, j))],
        out_specs=pl.BlockSpec((st.tm, tn), lambda i, j: (i, j)),
        out_shape=jax.ShapeDtypeStruct((rows, f), BF16),
        scratch_shapes=[pltpu.VMEM((st.tm + HALO, d), BF16)],
        compiler_params=_params("parallel", "arbitrary"),
        name="ffn_in",
    )(h, h, h, w_in, w_in, conv)


def _sc_in_kernel(hm_ref, hp_ref, hn_ref, wb_ref, wc_ref, wv_ref, cw_ref, o_ref, lhs_ref, *, tiles_per_seq):
    tm = hm_ref.shape[0]

    @pl.when(pl.program_id(1) == 0)
    def _():
        _fill_lhs(lhs_ref, hm_ref, hp_ref, hn_ref, tiles_per_seq)

    lhs = lhs_ref[...]
    cg = jnp.dot(lhs, wc_ref[...], preferred_element_type=F32)
    v = jnp.dot(lhs, wv_ref[...], preferred_element_type=F32)
    bg = jnp.dot(lhs_ref[HALO:HALO + tm, :], wb_ref[...], preferred_element_type=F32)
    o_ref[...] = (bg * _conv3(cg * v, cw_ref[...], tm)).astype(o_ref.dtype)


def _sc_in(h, w_in, conv, layer, st):
    rows, d = h.shape
    n = conv.shape[-1]
    tn = w_in.shape[-1]
    nj = n // tn
    kern = functools.partial(_sc_in_kernel, tiles_per_seq=st.tiles_per_seq)
    return pl.pallas_call(
        kern,
        grid=(st.n_tiles, nj),
        in_specs=_halo_specs(st, d) + [_w_spec(layer, d, tn), _w_spec(layer, d, tn, nj),
                                       _w_spec(layer, d, tn, 2 * nj),
                                       pl.BlockSpec((None, 3, tn), lambda i, j: (layer, 0, j))],
        out_specs=pl.BlockSpec((st.tm, tn), lambda i, j: (i, j)),
        out_shape=jax.ShapeDtypeStruct((rows, n), BF16),
        scratch_shapes=[pltpu.VMEM((st.tm + HALO, d), BF16)],
        compiler_params=_params("parallel", "arbitrary"),
        name="sc_in",
    )(h, h, h, w_in, w_in, w_in, conv)


def _block_scan(x, pos, block, reverse):
    n = x.shape[0]
    s = 1
    while s < block:
        if reverse:
            x = x + jnp.where(pos + s < block, pltpu.roll(x, n - s, 0), 0.0)
        else:
            x = x + jnp.where(pos >= s, pltpu.roll(x, s, 0), 0.0)
        s *= 2
    return x


def _gdn_in_kernel(hm_ref, hp_ref, hn_ref, w_ref, wab_ref, cw_ref, ab_ref, o_ref, gb_ref, gbt_ref, lhs_ref,
                   *, tiles_per_seq, n_qk_tiles, n_conv_tiles, n_heads):
    tm = hm_ref.shape[0]
    tn = w_ref.shape[1]
    j = pl.program_id(1)

    @pl.when(j == 0)
    def _():
        _fill_lhs(lhs_ref, hm_ref, hp_ref, hn_ref, tiles_per_seq)
        lane = lax.broadcasted_iota(jnp.int32, (1, LANES), 1)
        ab = jnp.dot(lhs_ref[HALO:HALO + tm, :], wab_ref[...], preferred_element_type=F32)
        ab = jnp.where(lane < 2 * N_DIRS * n_heads, ab, 0.0)
        p = ab_ref[...]
        x = ab + p[1:2]
        log_alpha = -jnp.exp(p[0:1]) * (jnp.maximum(x, 0.0) + jnp.log1p(jnp.exp(-jnp.abs(x))))
        beta = jax.nn.sigmoid(ab)
        pos = lax.broadcasted_iota(jnp.int32, (tm, 1), 0) & (SCAN_BLOCK - 1)
        g = jnp.where(lane < n_heads, _block_scan(log_alpha, pos, SCAN_BLOCK, False),
                      _block_scan(log_alpha, pos, SCAN_BLOCK, True))
        gb = jnp.where(lane < N_DIRS * n_heads, g, beta)
        gb_ref[...] = gb
        gbt_ref[...] = gb.T

    scale = jnp.where(j < n_qk_tiles // 2, HEAD_DIM ** -0.5, 1.0).astype(F32)
    sub = min(tn, COL_TILE)
    starts = range(0, tn, sub)
    parts = [_dot_row_chunks(lhs_ref, w_ref[:, c0:c0 + sub]) for c0 in starts]
    for c0, a in zip(starts, parts):
        x = jnp.where(j < n_conv_tiles, _silu(_conv3(a, cw_ref[:, c0:c0 + sub], tm)), a[HALO:HALO + tm])
        for hh in range(sub // HEAD_DIM):
            xh = x[:, hh * HEAD_DIM:(hh + 1) * HEAD_DIM]
            inv = lax.rsqrt(jnp.sum(xh * xh, axis=-1, keepdims=True) + NORM_EPS) * scale
            o_ref[c0 // HEAD_DIM + hh] = (xh * jnp.where(j < n_qk_tiles, inv, 1.0)).astype(o_ref.dtype)


def _gdn_in(h, w_in, w_ab, conv, layer, ab_par, st, n_heads):
    rows, d = h.shape
    n = 4 * d
    tn = w_in.shape[-1]
    nj = n // tn
    n_conv = conv.shape[-1] // tn
    kern = functools.partial(_gdn_in_kernel, tiles_per_seq=st.tiles_per_seq, n_qk_tiles=2 * d // tn,
                             n_conv_tiles=n_conv, n_heads=n_heads)
    hpt = tn // HEAD_DIM
    return pl.pallas_call(
        kern,
        grid=(st.n_tiles, nj),
        in_specs=_halo_specs(st, d) + [
            _w_spec(layer, d, tn),
            pl.BlockSpec((None, d, LANES), lambda i, j: (layer, 0, 0)),
            pl.BlockSpec((None, 3, tn), lambda i, j: (layer, 0, jnp.minimum(j, n_conv - 1))),
            pl.BlockSpec((2, LANES), lambda i, j: (0, 0))],
        out_specs=[pl.BlockSpec((hpt, st.tm, HEAD_DIM), lambda i, j: (j, i, 0)),
                   pl.BlockSpec((st.tm, LANES), lambda i, j: (i, 0)),
                   pl.BlockSpec((LANES, st.tm), lambda i, j: (0, i))],
        out_shape=[jax.ShapeDtypeStruct((n // HEAD_DIM, rows, HEAD_DIM), BF16),
                   jax.ShapeDtypeStruct((rows, LANES), F32),
                   jax.ShapeDtypeStruct((LANES, rows), F32)],
        scratch_shapes=[pltpu.VMEM((st.tm + HALO, d), BF16)],
        compiler_params=_params("parallel", "arbitrary"),
        name="gdn_in",
    )(h, h, h, w_in, w_ab, conv, ab_par)


def _each(f, *lists):
    return [f(*xs) for xs in zip(*lists)]


def _bdot(a, b):
    return _each(lambda x, y: jnp.dot(x.astype(BF16), y.astype(BF16), preferred_element_type=F32), a, b)


def _unit_tri_solve(neg_l, rhs, ri, ci, upper):
    c = neg_l[0].shape[0]
    base = 8

    def same_block(width):
        sh = width.bit_length() - 1
        return (ri >> sh) == (ci >> sh)

    diag = {up: jnp.logical_and(same_block(base), (ri < ci) if up else (ri > ci)) for up in set(upper)}
    n0 = [jnp.where(diag[up], m, 0.0) for m, up in zip(neg_l, upper)]
    p2 = _bdot(n0, n0)
    t = _bdot([jnp.concatenate([a, b], axis=0) for a, b in zip(n0, p2)], p2)
    x = _each(lambda a, b, tt: a + b + tt[0:c], n0, p2, t)
    p4 = [tt[c:2 * c] for tt in t]
    x = _each(lambda xx, p, u: xx + p + u, x, p4, _bdot(x, p4))
    size = base
    while size < c:
        pair = jnp.logical_and(same_block(2 * size), jnp.logical_not(same_block(size)))
        cm = [jnp.where(pair, m, 0.0) for m in neg_l]
        blocks = [range(0 if up else 1, c // size, 2) for up in upper]
        pick = lambda m, bl: jnp.concatenate([m[b * size:(b + 1) * size] for b in bl], axis=0)
        cm_h = _each(pick, cm, blocks)
        a_h = _each(lambda m, u: m + u, cm_h, _bdot(_each(pick, x, blocks), cm))
        a_h = _each(lambda aa, u: aa + u, a_h, _bdot(a_h, x))
        zero = jnp.zeros((size, c), F32)

        def spread(aa, bl):
            slabs = [aa[k * size:(k + 1) * size] for k in range(len(bl))]
            return jnp.concatenate([s for k in range(len(bl)) for s in
                                    ((slabs[k], zero) if bl[0] == 0 else (zero, slabs[k]))], axis=0)

        x = _each(lambda xx, aa, bl: xx + spread(aa, bl), x, a_h, blocks)
        size *= 2
    return _each(lambda r, u: r + u, rhs, _bdot(x, rhs))


def _delta_blocks(q, k, v, gcol, bcol, grow, s, reverse):
    c = q[0].shape[0]
    ri = lax.broadcasted_iota(jnp.int32, (c, c), 0)
    ci = lax.broadcasted_iota(jnp.int32, (c, c), 1)
    masks = {r: (ri <= ci) if r else (ri >= ci) for r in set(reverse)}
    seen = [masks[r] for r in reverse]
    g_end = [g[0:1] if r else g[c - 1:c] for g, r in zip(gcol, reverse)]
    kf = [x.astype(F32) for x in k]
    kbn = _each(lambda x, b: x * (-b), kf, bcol)
    eg = [jnp.exp(g) for g in gcol]
    kq = _each(lambda x, y, z: lax.dot_general(jnp.concatenate([x.astype(BF16), y], axis=0), z,
                                               (((1,), (1,)), ((), ())), preferred_element_type=F32), kbn, q, k)
    gam = _each(lambda m, gc, gr: jnp.exp(jnp.where(m, gc - gr, -jnp.inf)), seen, gcol, grow)
    neg_l = _each(lambda x, gm: x[0:c] * gm, kq, gam)
    qk = _each(lambda x, gm: x[c:2 * c] * gm, kq, gam)
    rhs = _each(lambda x, b, y, e: jnp.concatenate([x.astype(F32) * b, y * (-e)], axis=1), v, bcol, kbn, eg)
    y = _unit_tri_solve(neg_l, rhs, ri, ci, reverse)
    q_dec = _each(lambda x, e: x.astype(F32) * e, q, eg)
    k_dec = _each(lambda x, ge, g: x * jnp.exp(ge - g), kf, g_end, gcol)
    vs = _bdot(_each(lambda yy, qd: jnp.concatenate([yy[:, HEAD_DIM:2 * HEAD_DIM], qd], axis=0), y, q_dec), s)
    v_new = _each(lambda yy, u: (yy[:, 0:HEAD_DIM] - u[0:c]).astype(BF16), y, vs)
    o = _each(lambda u, w: u[c:2 * c] + w, vs, _bdot(qk, v_new))
    kv = _each(lambda x, w: lax.dot_general(x.astype(BF16), w, (((0,), (0,)), ((), ())),
                                            preferred_element_type=F32), k_dec, v_new)
    s_new = _each(lambda ss, ge, u: ss * jnp.exp(ge) + u, s, g_end, kv)
    return o, s_new


def _gdn_core_kernel(qf_ref, kf_ref, vf_ref, qb_ref, kb_ref, vb_ref, gbf_ref, gbb_ref, gtf_ref, gtb_ref,
                     s0_ref, of_ref, ob_ref, sout_ref, s_ref, *, n_heads):
    n = pl.program_id(2)
    heads = qf_ref.shape[0]
    h0 = pl.program_id(1) * heads

    @pl.when(n == 0)
    def _():
        s_ref[...] = s0_ref[...]

    lane = lax.broadcasted_iota(jnp.int32, (1, LANES), 1)
    dirs = ((qf_ref, kf_ref, vf_ref, gbf_ref, gtf_ref, of_ref, False),
            (qb_ref, kb_ref, vb_ref, gbb_ref, gtb_ref, ob_ref, True))
    q, k, v, gcol, bcol, grow, s, reverse, dest = [], [], [], [], [], [], [], [], []
    for d, (q_ref, k_ref, v_ref, gb_ref, gt_ref, o_ref, rev) in enumerate(dirs):
        gb = gb_ref[...]
        for hh in range(heads):
            col = d * n_heads + h0 + hh
            gcol.append(jnp.sum(jnp.where(lane == col, gb, 0.0), axis=1, keepdims=True))
            bcol.append(jnp.sum(jnp.where(lane == col + N_DIRS * n_heads, gb, 0.0), axis=1, keepdims=True))
            grow.append(gt_ref[pl.ds(col, 1), :])
            q.append(q_ref[hh])
            k.append(k_ref[hh])
            v.append(v_ref[hh])
            s.append(s_ref[hh, d])
            reverse.append(rev)
            dest.append((o_ref, hh, d))
    o, s_new = _delta_blocks(q, k, v, gcol, bcol, grow, s, reverse)
    for (o_ref, hh, d), oo, ss in zip(dest, o, s_new):
        o_ref[hh] = oo.astype(o_ref.dtype)
        s_ref[hh, d] = ss

    @pl.when(n == pl.num_programs(2) - 1)
    def _():
        sout_ref[...] = s_ref[...]


def _gdn_core(qkvg, gb, gbt, s0, n_seq, seq_len, n_heads, o_dtype):
    rows = n_seq * seq_len
    c = SCAN_BLOCK
    assert seq_len % c == 0
    nc = seq_len // c
    g = min(CORE_HEADS, n_heads)
    assert n_heads % g == 0
    ng = n_heads // g

    def fwd(b, n):
        return b * nc + n

    def bwd(b, n):
        return b * nc + nc - 1 - n

    def head_spec(part, blk):
        return pl.BlockSpec((g, c, HEAD_DIM), lambda b, hg, n: (part * ng + hg, blk(b, n), 0))

    state_spec = pl.BlockSpec((None, g, N_DIRS, HEAD_DIM, HEAD_DIM), lambda b, hg, n: (b, hg, 0, 0, 0))
    kern = functools.partial(_gdn_core_kernel, n_heads=n_heads)
    return pl.pallas_call(
        kern,
        grid=(n_seq, ng, nc),
        in_specs=[head_spec(0, fwd), head_spec(1, fwd), head_spec(2, fwd),
                  head_spec(0, bwd), head_spec(1, bwd), head_spec(2, bwd),
                  pl.BlockSpec((c, LANES), lambda b, hg, n: (fwd(b, n), 0)),
                  pl.BlockSpec((c, LANES), lambda b, hg, n: (bwd(b, n), 0)),
                  pl.BlockSpec((LANES, c), lambda b, hg, n: (0, fwd(b, n))),
                  pl.BlockSpec((LANES, c), lambda b, hg, n: (0, bwd(b, n))),
                  state_spec],
        out_specs=[pl.BlockSpec((g, c, HEAD_DIM), lambda b, hg, n: (hg, fwd(b, n), 0)),
                   pl.BlockSpec((g, c, HEAD_DIM), lambda b, hg, n: (hg, bwd(b, n), 0)),
                   state_spec],
        out_shape=[jax.ShapeDtypeStruct((n_heads, rows, HEAD_DIM), o_dtype),
                   jax.ShapeDtypeStruct((n_heads, rows, HEAD_DIM), o_dtype),
                   jax.ShapeDtypeStruct(s0.shape, F32)],
        scratch_shapes=[pltpu.VMEM((g, N_DIRS, HEAD_DIM, HEAD_DIM), F32)],
        compiler_params=_params("parallel", "parallel", "arbitrary"),
        name="gdn_core",
    )(qkvg, qkvg, qkvg, qkvg, qkvg, qkvg, gb, gb, gbt, gbt, s0)


def _gdn_combine_kernel(of_ref, ob_ref, gate_ref, g_ref, o_ref):
    for hh in range(of_ref.shape[0]):
        o = of_ref[hh].astype(F32) + ob_ref[hh].astype(F32)
        on = _rmsnorm(o, g_ref[...])
        o_ref[:, hh * HEAD_DIM:(hh + 1) * HEAD_DIM] = (on * _silu(gate_ref[hh].astype(F32))).astype(o_ref.dtype)


def _gdn_combine(o_f, o_b, qkvg, norm_g, tm):
    n_heads, rows, _ = o_f.shape
    g = min(CORE_HEADS, n_heads)
    ng = n_heads // g
    return pl.pallas_call(
        _gdn_combine_kernel,
        grid=(rows // tm, ng),
        in_specs=[pl.BlockSpec((g, tm, HEAD_DIM), lambda i, j: (j, i, 0)),
                  pl.BlockSpec((g, tm, HEAD_DIM), lambda i, j: (j, i, 0)),
                  pl.BlockSpec((g, tm, HEAD_DIM), lambda i, j: (3 * ng + j, i, 0)),
                  pl.BlockSpec((1, HEAD_DIM), lambda i, j: (0, 0))],
        out_specs=pl.BlockSpec((tm, g * HEAD_DIM), lambda i, j: (i, j)),
        out_shape=jax.ShapeDtypeStruct((rows, n_heads * HEAD_DIM), BF16),
        compiler_params=_params("parallel", "parallel"),
        name="gdn_combine",
    )(o_f, o_b, qkvg, norm_g.reshape(1, HEAD_DIM))


def _mm_out_kernel(*refs, gate_row, next_rows, has_scale, emit_y):
    it = iter(refs)
    lhs_ref, w_ref, y_ref, gate_ref = next(it), next(it), next(it), next(it)
    scale_ref = next(it) if has_scale else None
    gn_ref = next(it)
    modn_ref = next(it) if next_rows is not None else None
    ynew_ref = next(it) if emit_y else None
    hn_ref, row_ref = next(it), next(it)
    j = pl.program_id(1)
    nj, _, tn = row_ref.shape

    acc = jnp.dot(lhs_ref[...], w_ref[...], preferred_element_type=F32)
    if has_scale:
        acc = acc * scale_ref[...]
    y_new = y_ref[...] + gate_ref[gate_row:gate_row + 1, :] * acc
    if emit_y:
        ynew_ref[...] = y_new
    row_ref[j] = y_new

    @pl.when(j == nj - 1)
    def _():
        ss = jnp.zeros((row_ref.shape[1], 1), F32)
        for jj in range(nj):
            yj = row_ref[jj]
            ss = ss + jnp.sum(yj * yj, axis=-1, keepdims=True)
        inv = lax.rsqrt(ss / (nj * tn) + NORM_EPS)
        for jj in range(nj):
            cols = slice(jj * tn, (jj + 1) * tn)
            hj = row_ref[jj] * inv * gn_ref[:, cols]
            if next_rows is not None:
                sh, sc = next_rows
                hj = hj * (1.0 + modn_ref[sc:sc + 1, cols]) + modn_ref[sh:sh + 1, cols]
            hn_ref[:, cols] = hj.astype(hn_ref.dtype)


def _mm_out(lhs, w, w_layer, y, modr, layer, gate_row, st, next_g, next_mod, out_dtype, col_scale=None,
            emit_y=True):
    rows, n = y.shape
    grouped = w.ndim == 5
    k, tn = w.shape[-2:]
    nj = n // tn
    per_group = w.shape[-3]
    if grouped:
        lhs_spec = pl.BlockSpec((st.tm, k), lambda i, j: (i, j // per_group))
        w_spec = pl.BlockSpec((None, None, None, k, tn),
                              lambda i, j: (w_layer, j // per_group, j % per_group, 0, 0))
    else:
        lhs_spec = pl.BlockSpec((st.tm, k), lambda i, j: (i, 0))
        w_spec = pl.BlockSpec((None, None, k, tn), lambda i, j: (w_layer, j, 0, 0))
    in_specs = [lhs_spec, w_spec,
                pl.BlockSpec((st.tm, tn), lambda i, j: (i, j)),
                st.mod_spec(layer, n, tn)]
    args = [lhs, w, y, modr]
    if col_scale is not None:
        in_specs.append(pl.BlockSpec((1, tn), lambda i, j: (0, j)))
        args.append(col_scale.reshape(1, n))
    in_specs.append(pl.BlockSpec((1, n), lambda i, j: (0, 0)))
    args.append(next_g.reshape(1, n))
    next_rows = None
    if next_mod is not None:
        in_specs.append(st.mod_spec(next_mod[0], n))
        args.append(modr)
        next_rows = (next_mod[1], next_mod[2])
    out_specs, out_shape = [], []
    if emit_y:
        out_specs.append(pl.BlockSpec((st.tm, tn), lambda i, j: (i, j)))
        out_shape.append(jax.ShapeDtypeStruct((rows, n), F32))
    out_specs.append(pl.BlockSpec((st.tm, n), lambda i, j: (i, 0)))
    out_shape.append(jax.ShapeDtypeStruct((rows, n), out_dtype))
    kern = functools.partial(_mm_out_kernel, gate_row=gate_row, next_rows=next_rows,
                             has_scale=col_scale is not None, emit_y=emit_y)
    res = pl.pallas_call(
        kern,
        grid=(st.n_tiles, nj),
        in_specs=in_specs,
        out_specs=out_specs,
        out_shape=out_shape,
        scratch_shapes=[pltpu.VMEM((nj, st.tm, tn), F32)],
        compiler_params=_params("parallel", "arbitrary"),
        name="mm_out",
    )(*args)
    return res if emit_y else (None, res[0])


def kernel(x, c, ctx, c_ctx, mod_w, mod_b, norm1_g, norm2_g, ffn_w_in, ffn_conv, ffn_w_out, pool_w, pool_scale, sc_w_in, sc_conv, sc_w_out, gdn_w_in, gdn_conv, gdn_a_log, gdn_dt_bias, gdn_norm_g, gdn_w_out, final_g):
    b, t, d = x.shape
    t_ctx = ctx.shape[1]
    depth = mod_w.shape[0]
    n_mix = 3
    n_heads = d // HEAD_DIM
    assert 2 * N_DIRS * n_heads <= LANES
    gdn_layers = [i for i in range(depth) if i % n_mix == 2]
    last_ctx = gdn_layers[-1] if gdn_layers else -1

    lat = _Stream(b, t, min(ROW_TILE, t))
    cst = _Stream(b, t_ctx, min(ROW_TILE, t_ctx), ctx_row=b)
    lat.proj = _Stream(b, t, min(IN_ROW_TILE, t))
    cst.proj = _Stream(b, t_ctx, min(IN_ROW_TILE, t_ctx), ctx_row=b)
    modr = _mod_vectors(c, c_ctx, mod_w, mod_b)

    y = x.reshape(b * t, d)
    yc = ctx.reshape(b * t_ctx, d)

    def h_dtype(i):
        return F32 if i % n_mix == 0 else BF16

    h = _norm_mod(y, modr, 0, norm1_g[0], lat, h_dtype(0))
    hc = _norm_mod(yc, modr, 0, norm1_g[0], cst, h_dtype(0)) if last_ctx >= 0 else None

    def blocked(w, cap=None):
        return _col_blocked(w.astype(BF16), _col_tile(w.shape[-1], cap))

    n_ab = 2 * N_DIRS * n_heads
    gdn_w_ab = jnp.pad(gdn_w_in[..., 4 * d:], ((0, 0), (0, 0), (0, LANES - n_ab))).astype(BF16)
    gdn_w_in = _col_blocked(gdn_w_in[..., :4 * d].astype(BF16), _col_tile(d, GDN_COL_TILE))
    ffn_w_in = _col_blocked(ffn_w_in.astype(BF16), _col_tile(ffn_conv.shape[-1]))
    sc_w_in = _col_blocked(sc_w_in.astype(BF16), _col_tile(sc_conv.shape[-1]))
    ffn_w_out, sc_w_out, gdn_w_out = blocked(ffn_w_out), blocked(sc_w_out), blocked(gdn_w_out)
    pool_w = blocked(pool_w, WIDE_COL_TILE)

    out = None
    for i in range(depth):
        kind, jx = i % n_mix, i // n_mix
        ctx_stream = i < last_ctx
        last = i == depth - 1
        streams = [(lat, h, y, t, GRID_W)]
        if ctx_stream:
            streams.append((cst, hc, yc, t_ctx, None))

        if kind == 2:
            ab_par = jnp.pad(jnp.stack([gdn_a_log[jx].reshape(-1), gdn_dt_bias[jx].reshape(-1)]),
                             ((0, 0), (0, LANES - N_DIRS * n_heads)))
            qc, gbc, gbtc = _gdn_in(hc, gdn_w_in, gdn_w_ab, gdn_conv, jx, ab_par, cst, n_heads)
            s0 = jnp.zeros((b, n_heads, N_DIRS, HEAD_DIM, HEAD_DIM), F32)
            ocf, ocb, s_ctx = _gdn_core(qc, gbc, gbtc, s0, b, t_ctx, n_heads, BF16)
            ql, gbl, gbtl = _gdn_in(h, gdn_w_in, gdn_w_ab, gdn_conv, jx, ab_par, lat, n_heads)
            o_f, o_b, _ = _gdn_core(ql, gbl, gbtl, s_ctx, b, t, n_heads, BF16)
            mixed = [(_gdn_combine(o_f, o_b, ql, gdn_norm_g[jx], lat.tm), gdn_w_out, None)]
            if ctx_stream:
                mixed.append((_gdn_combine(ocf, ocb, qc, gdn_norm_g[jx], cst.tm), gdn_w_out, None))
        elif kind == 1:
            mixed = [(_sc_in(hs, sc_w_in, sc_conv, jx, st), sc_w_out, None) for st, hs, _, _, _ in streams]
        else:
            mixed = [(_pool_delta(hs, b, sl, gw), pool_w, pool_scale[jx]) for st, hs, _, sl, gw in streams]

        new = []
        for (st, _, ys, _, _), (m_lhs, w_out, scale) in zip(streams, mixed):
            y1, h2 = _mm_out(m_lhs, w_out, jx, ys, modr, i, 2, st, norm2_g[i], (i, 3, 4), BF16, col_scale=scale)
            f = _ffn_in(h2, ffn_w_in, ffn_conv, i, st.proj)
            if last:
                _, hn = _mm_out(f, ffn_w_out, i, y1, modr, i, 5, st, final_g, None, F32, emit_y=False)
                y2 = None
            else:
                y2, hn = _mm_out(f, ffn_w_out, i, y1, modr, i, 5, st, norm1_g[i + 1], (i + 1, 0, 1),
                                 h_dtype(i + 1))
            new.append((y2, hn))
        y, h = new[0]
        if ctx_stream:
            yc, hc = new[1]
        if last:
            out = h
    return out.reshape(b, t, d).astype(x.dtype)
```

```python
import functools
import math

import jax
import jax.numpy as jnp
from jax import lax
from jax.experimental import pallas as pl
from jax.experimental.pallas import tpu as pltpu

F32 = jnp.float32
BF16 = jnp.bfloat16

GRID_W = 64
POOL_WINDOWS = (2, 4, 8, 16)
HEAD_DIM = 128
N_DIRS = 2
NORM_EPS = 1e-6
SCAN_BLOCK = 128
HALO = 16
LANES = 128
VMEM_LIMIT_BYTES = 56 * 1024 * 1024
ROW_TILE = 512
IN_ROW_TILE = 1024
COL_TILE = 512
GDN_COL_TILE = 1024
WIDE_COL_TILE = 1024
DOT_ROWS = 128
POOL_COLS = 256
CORE_HEADS = 16


def _params(*sem):
    return pltpu.CompilerParams(dimension_semantics=sem, vmem_limit_bytes=VMEM_LIMIT_BYTES)


def _col_tile(n, cap=None):
    cap = COL_TILE if cap is None else cap
    for t in (cap, 512, 256, 128):
        if t <= cap and n % t == 0:
            return t
    raise ValueError(f"no column tile for {n}")


def _silu(x):
    return x * jax.nn.sigmoid(x)


def _rmsnorm(y, g):
    return y * lax.rsqrt(jnp.mean(y * y, axis=-1, keepdims=True) + NORM_EPS) * g


def _mod_kernel(c_ref, w_ref, b_ref, o_ref):
    s = _silu(c_ref[...]).astype(BF16)
    o_ref[...] = jnp.dot(s, w_ref[...].astype(BF16), preferred_element_type=F32) + b_ref[...]


def _mod_vectors(c, c_ctx, mod_w, mod_b):
    depth, d, d6 = mod_w.shape
    b = c.shape[0]
    r = -(-(b + 1) // 16) * 16
    cc = jnp.zeros((r, d), F32).at[:b].set(c).at[b].set(c_ctx)
    tn = _col_tile(d6)
    out = pl.pallas_call(
        _mod_kernel,
        grid=(depth, d6 // tn),
        in_specs=[pl.BlockSpec((r, d), lambda l, j: (0, 0)),
                  pl.BlockSpec((None, d, tn), lambda l, j: (l, 0, j)),
                  pl.BlockSpec((None, 1, tn), lambda l, j: (l, 0, j))],
        out_specs=pl.BlockSpec((None, r, tn), lambda l, j: (l, 0, j)),
        out_shape=jax.ShapeDtypeStruct((depth, r, d6), F32),
        compiler_params=_params("parallel", "parallel"),
        name="mod",
    )(cc, mod_w, mod_b.reshape(depth, 1, d6))
    return out.reshape(depth, r, 6, d)


class _Stream:
    def __init__(self, n_seq, seq_len, tm, ctx_row=None):
        self.n_seq, self.seq_len, self.tm = n_seq, seq_len, tm
        self.rows = n_seq * seq_len
        assert self.rows % tm == 0 and tm % HALO == 0
        if tm > seq_len:
            assert tm % seq_len == 0 and seq_len & (seq_len - 1) == 0 and ctx_row is not None
        else:
            assert seq_len % tm == 0
        self.tiles_per_seq = max(seq_len // tm, 1)
        self.seq_rows = seq_len if tm > seq_len else None
        self.n_tiles = self.rows // tm
        self.ctx_row = ctx_row

    @property
    def geom(self):
        return dict(tiles_per_seq=self.tiles_per_seq, seq_rows=self.seq_rows)

    def mod_row(self, i):
        return i // self.tiles_per_seq if self.ctx_row is None else self.ctx_row

    def mod_spec(self, layer, d, cols=None):
        if cols is None:
            return pl.BlockSpec((None, None, 6, d), lambda i, j: (layer, self.mod_row(i), 0, 0))
        return pl.BlockSpec((None, None, 6, cols), lambda i, j: (layer, self.mod_row(i), 0, j))


def _ctx_tile(cap, n_seq, seq_len):
    if seq_len >= cap or seq_len & (seq_len - 1):
        return min(cap, seq_len)
    per = max(p for p in range(1, n_seq + 1) if n_seq % p == 0 and p * seq_len <= cap)
    return per * seq_len


def _norm_kernel(y_ref, mod_ref, g_ref, o_ref):
    m = mod_ref[...]
    h = _rmsnorm(y_ref[...], g_ref[...]) * (1.0 + m[1:2]) + m[0:1]
    o_ref[...] = h.astype(o_ref.dtype)


def _norm_mod(y, modr, layer, g, st, out_dtype):
    rows, d = y.shape
    return pl.pallas_call(
        _norm_kernel,
        grid=(st.n_tiles, 1),
        in_specs=[pl.BlockSpec((st.tm, d), lambda i, j: (i, 0)),
                  st.mod_spec(layer, d),
                  pl.BlockSpec((1, d), lambda i, j: (0, 0))],
        out_specs=pl.BlockSpec((st.tm, d), lambda i, j: (i, 0)),
        out_shape=jax.ShapeDtypeStruct((rows, d), out_dtype),
        compiler_params=_params("parallel", "arbitrary"),
        name="norm_mod",
    )(y, modr, g.reshape(1, d))


def _window_sum(x, pos, n, unit, w):
    tot = x.shape[0]
    fwd, bwd = w - w // 2, w // 2
    assert fwd & (fwd - 1) == 0 and bwd & (bwd - 1) == 0 and bwd >= 1
    a, s = x, 1
    while s < fwd:
        a = a + jnp.where(pos + s < n, pltpu.roll(a, tot - s * unit, 0), 0.0)
        s *= 2
    b, s = jnp.where(pos >= 1, pltpu.roll(x, unit, 0), 0.0), 1
    while s < bwd:
        b = b + jnp.where(pos - s >= 0, pltpu.roll(b, s * unit, 0), 0.0)
        s *= 2
    return a + b


def _window_sum_strided(x, unit, w):
    assert w & (w - 1) == 0 and unit % 8 == 0
    pad = jnp.zeros(((w // 2) * unit, x.shape[1]), x.dtype)
    s, k = jnp.concatenate([pad, x, pad], axis=0), 1
    while k < w:
        m = s.shape[0] - k * unit
        s = s[0:m] + s[k * unit:k * unit + m]
        k *= 2
    return s[0:x.shape[0]]


def _window_count(pos, n, w):
    return (jnp.minimum(pos + (w - w // 2), n) - jnp.maximum(pos - w // 2, 0)).astype(F32)


def _pool_kernel(h_ref, o_ref, *, tiles_per_group, grid_w, grid_rows):
    group = pl.program_id(1) // tiles_per_group
    t = h_ref.shape[0]
    tok = lax.broadcasted_iota(jnp.int32, (t, 1), 0)
    for gi, w in enumerate(POOL_WINDOWS):
        @pl.when(group == gi)
        def _(w=w):
            x = h_ref[...]
            if grid_rows is None:
                s = _window_sum(x, tok, t, 1, w)
                cnt = _window_count(tok, t, w)
            else:
                shift = grid_w.bit_length() - 1
                col, row = tok & (grid_w - 1), tok >> shift
                s = _window_sum_strided(x, grid_w, w)
                s = _window_sum(s, col, grid_w, 1, w)
                cnt = _window_count(row, grid_rows, w) * _window_count(col, grid_w, w)
            o_ref[...] = (s / cnt - x).astype(o_ref.dtype)


def _pool_delta(h, n_seq, seq_len, grid_w):
    rows, d = h.shape
    group = d // len(POOL_WINDOWS)
    ct = min(POOL_COLS, group)
    assert group % ct == 0
    if grid_w is not None:
        assert grid_w & (grid_w - 1) == 0 and seq_len % grid_w == 0
    kern = functools.partial(_pool_kernel, tiles_per_group=group // ct, grid_w=grid_w,
                             grid_rows=None if grid_w is None else seq_len // grid_w)
    return pl.pallas_call(
        kern,
        grid=(n_seq, d // ct),
        in_specs=[pl.BlockSpec((seq_len, ct), lambda b, j: (b, j))],
        out_specs=pl.BlockSpec((seq_len, ct), lambda b, j: (b, j)),
        out_shape=jax.ShapeDtypeStruct((rows, d), BF16),
        compiler_params=_params("parallel", "parallel"),
        name="pool",
    )(h)


def _halo_specs(st, d):
    per = st.tm // HALO
    last = st.rows // HALO - 1
    return [pl.BlockSpec((st.tm, d), lambda i, j: (i, 0)),
            pl.BlockSpec((HALO, d), lambda i, j: (jnp.maximum(i * per - 1, 0), 0)),
            pl.BlockSpec((HALO, d), lambda i, j: (jnp.minimum((i + 1) * per, last), 0))]


def _w_spec(layer, d, tn, offset=0):
    return pl.BlockSpec((None, d, tn), lambda i, j: (layer, 0, j + offset))


def _fill_lhs(lhs_ref, hm_ref, hp_ref, hn_ref, tiles_per_seq):
    tm = hm_ref.shape[0]
    i = pl.program_id(0) % tiles_per_seq
    r = lax.broadcasted_iota(jnp.int32, hp_ref.shape, 0)
    nxt = jnp.where(i == tiles_per_seq - 1, jnp.zeros_like(hn_ref[...]), hn_ref[...])
    prv = jnp.where(i == 0, jnp.zeros_like(hp_ref[...]), hp_ref[...])
    lhs_ref[0:HALO, :] = jnp.where(r < HALO // 2, nxt, prv)
    lhs_ref[HALO:HALO + tm, :] = hm_ref[...]


def _dot_row_chunks(lhs_ref, w, first=0):
    n = lhs_ref.shape[0]
    bounds = [first] + list(range(first + (n - first) % DOT_ROWS + DOT_ROWS, n, DOT_ROWS)) + [n]
    parts = [jnp.dot(lhs_ref[r0:r1, :], w, preferred_element_type=F32) for r0, r1 in zip(bounds, bounds[1:])]
    return parts[0] if len(parts) == 1 else jnp.concatenate(parts, axis=0)


def _conv3(a_ext, cw, tm, seq_rows):
    n = a_ext.shape[0]
    dn = pltpu.roll(a_ext, 1, 0)[HALO:HALO + tm]
    up = pltpu.roll(a_ext, n - 1, 0)[HALO:HALO + tm]
    if seq_rows is not None:
        pos = lax.broadcasted_iota(jnp.int32, (tm, 1), 0) & (seq_rows - 1)
        dn = jnp.where(pos == 0, 0.0, dn)
        up = jnp.where(pos == seq_rows - 1, 0.0, up)
    return cw[0:1] * dn + cw[1:2] * a_ext[HALO:HALO + tm] + cw[2:3] * up


def _ffn_in_kernel(hm_ref, hp_ref, hn_ref, wa_ref, wu_ref, cw_ref, o_ref, lhs_ref, *, tiles_per_seq, seq_rows):
    tm = hm_ref.shape[0]

    @pl.when(pl.program_id(1) == 0)
    def _():
        _fill_lhs(lhs_ref, hm_ref, hp_ref, hn_ref, tiles_per_seq)

    a = jnp.dot(lhs_ref[...], wa_ref[...], preferred_element_type=F32)
    u = jnp.dot(lhs_ref[HALO:HALO + tm, :], wu_ref[...], preferred_element_type=F32)
    o_ref[...] = (_silu(_conv3(a, cw_ref[...], tm, seq_rows)) * u).astype(o_ref.dtype)


def _ffn_in(h, w_in, conv, layer, st):
    rows, d = h.shape
    f = conv.shape[-1]
    tn = _col_tile(f)
    nj = f // tn
    kern = functools.partial(_ffn_in_kernel, **st.geom)
    return pl.pallas_call(
        kern,
        grid=(st.n_tiles, nj),
        in_specs=_halo_specs(st, d) + [_w_spec(layer, d, tn), _w_spec(layer, d, tn, nj),
                                       pl.BlockSpec((None, 3, tn), lambda i, j: (layer, 0, j))],
        out_specs=pl.BlockSpec((st.tm, tn), lambda i, j: (i, j)),
        out_shape=jax.ShapeDtypeStruct((rows, f), BF16),
        scratch_shapes=[pltpu.VMEM((st.tm + HALO, d), BF16)],
        compiler_params=_params("parallel", "arbitrary"),
        name="ffn_in",
    )(h, h, h, w_in, w_in, conv)


def _sc_in_kernel(hm_ref, hp_ref, hn_ref, wb_ref, wc_ref, wv_ref, cw_ref, o_ref, lhs_ref, *, tiles_per_seq,
                  seq_rows):
    tm = hm_ref.shape[0]

    @pl.when(pl.program_id(1) == 0)
    def _():
        _fill_lhs(lhs_ref, hm_ref, hp_ref, hn_ref, tiles_per_seq)

    lhs = lhs_ref[...]
    cg = jnp.dot(lhs, wc_ref[...], preferred_element_type=F32)
    v = jnp.dot(lhs, wv_ref[...], preferred_element_type=F32)
    bg = jnp.dot(lhs_ref[HALO:HALO + tm, :], wb_ref[...], preferred_element_type=F32)
    o_ref[...] = (bg * _conv3(cg * v, cw_ref[...], tm, seq_rows)).astype(o_ref.dtype)


def _sc_in(h, w_in, conv, layer, st):
    rows, d = h.shape
    n = conv.shape[-1]
    tn = _col_tile(n)
    nj = n // tn
    kern = functools.partial(_sc_in_kernel, **st.geom)
    return pl.pallas_call(
        kern,
        grid=(st.n_tiles, nj),
        in_specs=_halo_specs(st, d) + [_w_spec(layer, d, tn), _w_spec(layer, d, tn, nj),
                                       _w_spec(layer, d, tn, 2 * nj),
                                       pl.BlockSpec((None, 3, tn), lambda i, j: (layer, 0, j))],
        out_specs=pl.BlockSpec((st.tm, tn), lambda i, j: (i, j)),
        out_shape=jax.ShapeDtypeStruct((rows, n), BF16),
        scratch_shapes=[pltpu.VMEM((st.tm + HALO, d), BF16)],
        compiler_params=_params("parallel", "arbitrary"),
        name="sc_in",
    )(h, h, h, w_in, w_in, w_in, conv)


def _block_scan(x, pos, block, reverse):
    n = x.shape[0]
    s = 1
    while s < block:
        if reverse:
            x = x + jnp.where(pos + s < block, pltpu.roll(x, n - s, 0), 0.0)
        else:
            x = x + jnp.where(pos >= s, pltpu.roll(x, s, 0), 0.0)
        s *= 2
    return x


def _gdn_in_kernel(hm_ref, hp_ref, hn_ref, w_ref, wab_ref, cw_ref, ab_ref, o_ref, gb_ref, gbt_ref, lhs_ref,
                   *, tiles_per_seq, seq_rows, n_qk_tiles, n_conv_tiles, n_heads):
    tm = hm_ref.shape[0]
    tn = w_ref.shape[1]
    j = pl.program_id(1)

    @pl.when(j == 0)
    def _():
        _fill_lhs(lhs_ref, hm_ref, hp_ref, hn_ref, tiles_per_seq)
        lane = lax.broadcasted_iota(jnp.int32, (1, LANES), 1)
        ab = jnp.dot(lhs_ref[HALO:HALO + tm, :], wab_ref[...], preferred_element_type=F32)
        ab = jnp.where(lane < 2 * N_DIRS * n_heads, ab, 0.0)
        p = ab_ref[...]
        x = ab + p[1:2]
        log_alpha = -jnp.exp(p[0:1]) * (jnp.maximum(x, 0.0) + jnp.log1p(jnp.exp(-jnp.abs(x))))
        beta = jax.nn.sigmoid(ab)
        pos = lax.broadcasted_iota(jnp.int32, (tm, 1), 0) & (SCAN_BLOCK - 1)
        g = jnp.where(lane < n_heads, _block_scan(log_alpha, pos, SCAN_BLOCK, False),
                      _block_scan(log_alpha, pos, SCAN_BLOCK, True))
        gb = jnp.where(lane < N_DIRS * n_heads, g, beta)
        gb_ref[...] = gb
        gbt_ref[...] = gb.T

    scale = jnp.where(j < n_qk_tiles // 2, HEAD_DIM ** -0.5, 1.0).astype(F32)
    sub = min(tn, COL_TILE)
    starts = range(0, tn, sub)
    parts = [_dot_row_chunks(lhs_ref, w_ref[:, c0:c0 + sub]) for c0 in starts]
    for c0, a in zip(starts, parts):
        x = jnp.where(j < n_conv_tiles, _silu(_conv3(a, cw_ref[:, c0:c0 + sub], tm, seq_rows)), a[HALO:HALO + tm])
        for hh in range(sub // HEAD_DIM):
            xh = x[:, hh * HEAD_DIM:(hh + 1) * HEAD_DIM]
            inv = lax.rsqrt(jnp.sum(xh * xh, axis=-1, keepdims=True) + NORM_EPS) * scale
            o_ref[c0 // HEAD_DIM + hh] = (xh * jnp.where(j < n_qk_tiles, inv, 1.0)).astype(o_ref.dtype)


def _gdn_in(h, w_in, conv, layer, ab_par, st, n_heads):
    rows, d = h.shape
    n = 4 * d
    assert w_in.shape[-1] - n <= LANES
    tn = _col_tile(d, GDN_COL_TILE)
    nj = n // tn
    n_conv = conv.shape[-1] // tn
    kern = functools.partial(_gdn_in_kernel, **st.geom, n_qk_tiles=2 * d // tn,
                             n_conv_tiles=n_conv, n_heads=n_heads)
    hpt = tn // HEAD_DIM
    return pl.pallas_call(
        kern,
        grid=(st.n_tiles, nj),
        in_specs=_halo_specs(st, d) + [
            _w_spec(layer, d, tn),
            pl.BlockSpec((None, d, LANES), lambda i, j: (layer, 0, n // LANES)),
            pl.BlockSpec((None, 3, tn), lambda i, j: (layer, 0, jnp.minimum(j, n_conv - 1))),
            pl.BlockSpec((2, LANES), lambda i, j: (0, 0))],
        out_specs=[pl.BlockSpec((hpt, st.tm, HEAD_DIM), lambda i, j: (j, i, 0)),
                   pl.BlockSpec((st.tm, LANES), lambda i, j: (i, 0)),
                   pl.BlockSpec((LANES, st.tm), lambda i, j: (0, i))],
        out_shape=[jax.ShapeDtypeStruct((n // HEAD_DIM, rows, HEAD_DIM), BF16),
                   jax.ShapeDtypeStruct((rows, LANES), F32),
                   jax.ShapeDtypeStruct((LANES, rows), F32)],
        scratch_shapes=[pltpu.VMEM((st.tm + HALO, d), BF16)],
        compiler_params=_params("parallel", "arbitrary"),
        name="gdn_in",
    )(h, h, h, w_in, w_in, conv, ab_par)


def _each(f, *lists):
    return [f(*xs) for xs in zip(*lists)]


def _bdot(a, b):
    return _each(lambda x, y: jnp.dot(x.astype(BF16), y.astype(BF16), preferred_element_type=F32), a, b)


def _unit_tri_solve(neg_l, rhs, ri, ci, upper):
    c = neg_l[0].shape[0]
    base = 8

    def same_block(width):
        sh = width.bit_length() - 1
        return (ri >> sh) == (ci >> sh)

    diag = {up: jnp.logical_and(same_block(base), (ri < ci) if up else (ri > ci)) for up in set(upper)}
    n0 = [jnp.where(diag[up], m, 0.0) for m, up in zip(neg_l, upper)]
    p2 = _bdot(n0, n0)
    t = _bdot([jnp.concatenate([a, b], axis=0) for a, b in zip(n0, p2)], p2)
    x = _each(lambda a, b, tt: a + b + tt[0:c], n0, p2, t)
    p4 = [tt[c:2 * c] for tt in t]
    x = _each(lambda xx, p, u: xx + p + u, x, p4, _bdot(x, p4))
    size = base
    while size < c:
        pair = jnp.logical_and(same_block(2 * size), jnp.logical_not(same_block(size)))
        cm = [jnp.where(pair, m, 0.0) for m in neg_l]
        blocks = [range(0 if up else 1, c // size, 2) for up in upper]
        pick = lambda m, bl: jnp.concatenate([m[b * size:(b + 1) * size] for b in bl], axis=0)
        cm_h = _each(pick, cm, blocks)
        a_h = _each(lambda m, u: m + u, cm_h, _bdot(_each(pick, x, blocks), cm))
        a_h = _each(lambda aa, u: aa + u, a_h, _bdot(a_h, x))
        zero = jnp.zeros((size, c), F32)

        def spread(aa, bl):
            slabs = [aa[k * size:(k + 1) * size] for k in range(len(bl))]
            return jnp.concatenate([s for k in range(len(bl)) for s in
                                    ((slabs[k], zero) if bl[0] == 0 else (zero, slabs[k]))], axis=0)

        x = _each(lambda xx, aa, bl: xx + spread(aa, bl), x, a_h, blocks)
        size *= 2
    return _each(lambda r, u: r + u, rhs, _bdot(x, rhs))


def _delta_blocks(q, k, v, gcol, bcol, grow, s, reverse):
    c = q[0].shape[0]
    ri = lax.broadcasted_iota(jnp.int32, (c, c), 0)
    ci = lax.broadcasted_iota(jnp.int32, (c, c), 1)
    masks = {r: (ri <= ci) if r else (ri >= ci) for r in set(reverse)}
    seen = [masks[r] for r in reverse]
    g_end = [g[0:1] if r else g[c - 1:c] for g, r in zip(gcol, reverse)]
    kf = [x.astype(F32) for x in k]
    kbn = _each(lambda x, b: x * (-b), kf, bcol)
    eg = [jnp.exp(g) for g in gcol]
    kq = _each(lambda x, y, z: lax.dot_general(jnp.concatenate([x.astype(BF16), y], axis=0), z,
                                               (((1,), (1,)), ((), ())), preferred_element_type=F32), kbn, q, k)
    gam = _each(lambda m, gc, gr: jnp.exp(jnp.where(m, gc - gr, -jnp.inf)), seen, gcol, grow)
    neg_l = _each(lambda x, gm: x[0:c] * gm, kq, gam)
    qk = _each(lambda x, gm: x[c:2 * c] * gm, kq, gam)
    rhs = _each(lambda x, b, y, e: jnp.concatenate([x.astype(F32) * b, y * (-e)], axis=1), v, bcol, kbn, eg)
    y = _unit_tri_solve(neg_l, rhs, ri, ci, reverse)
    q_dec = _each(lambda x, e: x.astype(F32) * e, q, eg)
    k_dec = _each(lambda x, ge, g: x * jnp.exp(ge - g), kf, g_end, gcol)
    vs = _bdot(_each(lambda yy, qd: jnp.concatenate([yy[:, HEAD_DIM:2 * HEAD_DIM], qd], axis=0), y, q_dec), s)
    v_new = _each(lambda yy, u: (yy[:, 0:HEAD_DIM] - u[0:c]).astype(BF16), y, vs)
    o = _each(lambda u, w: u[c:2 * c] + w, vs, _bdot(qk, v_new))
    kv = _each(lambda x, w: lax.dot_general(x.astype(BF16), w, (((0,), (0,)), ((), ())),
                                            preferred_element_type=F32), k_dec, v_new)
    s_new = _each(lambda ss, ge, u: ss * jnp.exp(ge) + u, s, g_end, kv)
    return o, s_new


def _gdn_core_kernel(qf_ref, kf_ref, vf_ref, qb_ref, kb_ref, vb_ref, gbf_ref, gbb_ref, gtf_ref, gtb_ref,
                     s0_ref, of_ref, ob_ref, sout_ref, s_ref, *, n_heads):
    n = pl.program_id(2)
    heads = qf_ref.shape[0]
    h0 = pl.program_id(1) * heads

    @pl.when(n == 0)
    def _():
        s_ref[...] = s0_ref[...]

    lane = lax.broadcasted_iota(jnp.int32, (1, LANES), 1)
    dirs = ((qf_ref, kf_ref, vf_ref, gbf_ref, gtf_ref, of_ref, False),
            (qb_ref, kb_ref, vb_ref, gbb_ref, gtb_ref, ob_ref, True))
    q, k, v, gcol, bcol, grow, s, reverse, dest = [], [], [], [], [], [], [], [], []
    for d, (q_ref, k_ref, v_ref, gb_ref, gt_ref, o_ref, rev) in enumerate(dirs):
        gb = gb_ref[...]
        for hh in range(heads):
            col = d * n_heads + h0 + hh
            gcol.append(jnp.sum(jnp.where(lane == col, gb, 0.0), axis=1, keepdims=True))
            bcol.append(jnp.sum(jnp.where(lane == col + N_DIRS * n_heads, gb, 0.0), axis=1, keepdims=True))
            grow.append(gt_ref[pl.ds(col, 1), :])
            q.append(q_ref[hh])
            k.append(k_ref[hh])
            v.append(v_ref[hh])
            s.append(s_ref[hh, d])
            reverse.append(rev)
            dest.append((o_ref, hh, d))
    o, s_new = _delta_blocks(q, k, v, gcol, bcol, grow, s, reverse)
    for (o_ref, hh, d), oo, ss in zip(dest, o, s_new):
        o_ref[hh] = oo.astype(o_ref.dtype)
        s_ref[hh, d] = ss

    @pl.when(n == pl.num_programs(2) - 1)
    def _():
        sout_ref[...] = s_ref[...]


def _gdn_core(qkvg, gb, gbt, s0, n_seq, seq_len, n_heads, o_dtype):
    rows = n_seq * seq_len
    c = SCAN_BLOCK
    assert seq_len % c == 0
    nc = seq_len // c
    g = min(CORE_HEADS, n_heads)
    assert n_heads % g == 0
    ng = n_heads // g

    def fwd(b, n):
        return b * nc + n

    def bwd(b, n):
        return b * nc + nc - 1 - n

    def head_spec(part, blk):
        return pl.BlockSpec((g, c, HEAD_DIM), lambda b, hg, n: (part * ng + hg, blk(b, n), 0))

    state_spec = pl.BlockSpec((None, g, N_DIRS, HEAD_DIM, HEAD_DIM), lambda b, hg, n: (b, hg, 0, 0, 0))
    kern = functools.partial(_gdn_core_kernel, n_heads=n_heads)
    return pl.pallas_call(
        kern,
        grid=(n_seq, ng, nc),
        in_specs=[head_spec(0, fwd), head_spec(1, fwd), head_spec(2, fwd),
                  head_spec(0, bwd), head_spec(1, bwd), head_spec(2, bwd),
                  pl.BlockSpec((c, LANES), lambda b, hg, n: (fwd(b, n), 0)),
                  pl.BlockSpec((c, LANES), lambda b, hg, n: (bwd(b, n), 0)),
                  pl.BlockSpec((LANES, c), lambda b, hg, n: (0, fwd(b, n))),
                  pl.BlockSpec((LANES, c), lambda b, hg, n: (0, bwd(b, n))),
                  state_spec],
        out_specs=[pl.BlockSpec((g, c, HEAD_DIM), lambda b, hg, n: (hg, fwd(b, n), 0)),
                   pl.BlockSpec((g, c, HEAD_DIM), lambda b, hg, n: (hg, bwd(b, n), 0)),
                   state_spec],
        out_shape=[jax.ShapeDtypeStruct((n_heads, rows, HEAD_DIM), o_dtype),
                   jax.ShapeDtypeStruct((n_heads, rows, HEAD_DIM), o_dtype),
                   jax.ShapeDtypeStruct(s0.shape, F32)],
        scratch_shapes=[pltpu.VMEM((g, N_DIRS, HEAD_DIM, HEAD_DIM), F32)],
        compiler_params=_params("parallel", "parallel", "arbitrary"),
        name="gdn_core",
    )(qkvg, qkvg, qkvg, qkvg, qkvg, qkvg, gb, gb, gbt, gbt, s0)


def _gdn_combine_kernel(of_ref, ob_ref, gate_ref, g_ref, o_ref):
    for hh in range(of_ref.shape[0]):
        o = of_ref[hh].astype(F32) + ob_ref[hh].astype(F32)
        on = _rmsnorm(o, g_ref[...])
        o_ref[:, hh * HEAD_DIM:(hh + 1) * HEAD_DIM] = (on * _silu(gate_ref[hh].astype(F32))).astype(o_ref.dtype)


def _gdn_combine(o_f, o_b, qkvg, norm_g, tm):
    n_heads, rows, _ = o_f.shape
    g = min(CORE_HEADS, n_heads)
    ng = n_heads // g
    return pl.pallas_call(
        _gdn_combine_kernel,
        grid=(rows // tm, ng),
        in_specs=[pl.BlockSpec((g, tm, HEAD_DIM), lambda i, j: (j, i, 0)),
                  pl.BlockSpec((g, tm, HEAD_DIM), lambda i, j: (j, i, 0)),
                  pl.BlockSpec((g, tm, HEAD_DIM), lambda i, j: (3 * ng + j, i, 0)),
                  pl.BlockSpec((1, HEAD_DIM), lambda i, j: (0, 0))],
        out_specs=pl.BlockSpec((tm, g * HEAD_DIM), lambda i, j: (i, j)),
        out_shape=jax.ShapeDtypeStruct((rows, n_heads * HEAD_DIM), BF16),
        compiler_params=_params("parallel", "parallel"),
        name="gdn_combine",
    )(o_f, o_b, qkvg, norm_g.reshape(1, HEAD_DIM))


def _mm_out_kernel(*refs, gate_row, next_rows, has_scale, emit_y):
    it = iter(refs)
    lhs_ref, w_ref, y_ref, gate_ref = next(it), next(it), next(it), next(it)
    scale_ref = next(it) if has_scale else None
    gn_ref = next(it)
    modn_ref = next(it) if next_rows is not None else None
    ynew_ref = next(it) if emit_y else None
    hn_ref, row_ref = next(it), next(it)
    j = pl.program_id(1)
    nj, _, tn = row_ref.shape

    acc = jnp.dot(lhs_ref[...], w_ref[...], preferred_element_type=F32)
    if has_scale:
        acc = acc * scale_ref[...]
    y_new = y_ref[...] + gate_ref[gate_row:gate_row + 1, :] * acc
    if emit_y:
        ynew_ref[...] = y_new
    row_ref[j] = y_new

    @pl.when(j == nj - 1)
    def _():
        ss = jnp.zeros((row_ref.shape[1], 1), F32)
        for jj in range(nj):
            yj = row_ref[jj]
            ss = ss + jnp.sum(yj * yj, axis=-1, keepdims=True)
        inv = lax.rsqrt(ss / (nj * tn) + NORM_EPS)
        for jj in range(nj):
            cols = slice(jj * tn, (jj + 1) * tn)
            hj = row_ref[jj] * inv * gn_ref[:, cols]
            if next_rows is not None:
                sh, sc = next_rows
                hj = hj * (1.0 + modn_ref[sc:sc + 1, cols]) + modn_ref[sh:sh + 1, cols]
            hn_ref[:, cols] = hj.astype(hn_ref.dtype)


def _mm_out(lhs, w, w_layer, y, modr, layer, gate_row, st, next_g, next_mod, out_dtype, col_scale=None,
            emit_y=True):
    rows, n = y.shape
    grouped = w.ndim == 4
    k = w.shape[-2]
    group_n = w.shape[-1]
    tn = _col_tile(group_n, WIDE_COL_TILE if grouped else None)
    nj = n // tn
    per_group = group_n // tn
    if grouped:
        lhs_spec = pl.BlockSpec((st.tm, k), lambda i, j: (i, j // per_group))
        w_spec = pl.BlockSpec((None, None, k, tn), lambda i, j: (w_layer, j // per_group, 0, j % per_group))
    else:
        lhs_spec = pl.BlockSpec((st.tm, k), lambda i, j: (i, 0))
        w_spec = pl.BlockSpec((None, k, tn), lambda i, j: (w_layer, 0, j))
    in_specs = [lhs_spec, w_spec,
                pl.BlockSpec((st.tm, tn), lambda i, j: (i, j)),
                st.mod_spec(layer, n, tn)]
    args = [lhs, w, y, modr]
    if col_scale is not None:
        in_specs.append(pl.BlockSpec((1, tn), lambda i, j: (0, j)))
        args.append(col_scale.reshape(1, n))
    in_specs.append(pl.BlockSpec((1, n), lambda i, j: (0, 0)))
    args.append(next_g.reshape(1, n))
    next_rows = None
    if next_mod is not None:
        in_specs.append(st.mod_spec(next_mod[0], n))
        args.append(modr)
        next_rows = (next_mod[1], next_mod[2])
    out_specs, out_shape = [], []
    if emit_y:
        out_specs.append(pl.BlockSpec((st.tm, tn), lambda i, j: (i, j)))
        out_shape.append(jax.ShapeDtypeStruct((rows, n), F32))
    out_specs.append(pl.BlockSpec((st.tm, n), lambda i, j: (i, 0)))
    out_shape.append(jax.ShapeDtypeStruct((rows, n), out_dtype))
    kern = functools.partial(_mm_out_kernel, gate_row=gate_row, next_rows=next_rows,
                             has_scale=col_scale is not None, emit_y=emit_y)
    res = pl.pallas_call(
        kern,
        grid=(st.n_tiles, nj),
        in_specs=in_specs,
        out_specs=out_specs,
        out_shape=out_shape,
        scratch_shapes=[pltpu.VMEM((nj, st.tm, tn), F32)],
        compiler_params=_params("parallel", "arbitrary"),
        name="mm_out",
    )(*args)
    return res if emit_y else (None, res[0])


def kernel(x, c, ctx, c_ctx, mod_w, mod_b, norm1_g, norm2_g, ffn_w_in, ffn_conv, ffn_w_out, pool_w, pool_scale, sc_w_in, sc_conv, sc_w_out, gdn_w_in, gdn_conv, gdn_a_log, gdn_dt_bias, gdn_norm_g, gdn_w_out, final_g):
    b, t, d = x.shape
    t_ctx = ctx.shape[1]
    depth = mod_w.shape[0]
    n_mix = 3
    n_heads = d // HEAD_DIM
    assert 2 * N_DIRS * n_heads <= LANES
    gdn_layers = [i for i in range(depth) if i % n_mix == 2]
    last_ctx = gdn_layers[-1] if gdn_layers else -1

    lat = _Stream(b, t, min(ROW_TILE, t))
    cst = _Stream(b, t_ctx, _ctx_tile(ROW_TILE, b, t_ctx), ctx_row=b)
    lat.proj = _Stream(b, t, min(IN_ROW_TILE, t))
    cst.proj = _Stream(b, t_ctx, _ctx_tile(IN_ROW_TILE, b, t_ctx), ctx_row=b)
    modr = _mod_vectors(c, c_ctx, mod_w, mod_b)

    y = x.reshape(b * t, d)
    yc = ctx.reshape(b * t_ctx, d)

    def h_dtype(i):
        return F32 if i % n_mix == 0 else BF16

    h = _norm_mod(y, modr, 0, norm1_g[0], lat, h_dtype(0))
    hc = _norm_mod(yc, modr, 0, norm1_g[0], cst, h_dtype(0)) if last_ctx >= 0 else None

    ffn_w_in, ffn_w_out, pool_w, sc_w_in, sc_w_out, gdn_w_in, gdn_w_out = (
        w.astype(BF16) for w in (ffn_w_in, ffn_w_out, pool_w, sc_w_in, sc_w_out, gdn_w_in, gdn_w_out))

    out = None
    for i in range(depth):
        kind, jx = i % n_mix, i // n_mix
        ctx_stream = i < last_ctx
        last = i == depth - 1
        streams = [(lat, h, y, t, GRID_W)]
        if ctx_stream:
            streams.append((cst, hc, yc, t_ctx, None))

        if kind == 2:
            ab_par = jnp.pad(jnp.stack([gdn_a_log[jx].reshape(-1), gdn_dt_bias[jx].reshape(-1)]),
                             ((0, 0), (0, LANES - N_DIRS * n_heads)))
            qc, gbc, gbtc = _gdn_in(hc, gdn_w_in, gdn_conv, jx, ab_par, cst, n_heads)
            s0 = jnp.zeros((b, n_heads, N_DIRS, HEAD_DIM, HEAD_DIM), F32)
            ocf, ocb, s_ctx = _gdn_core(qc, gbc, gbtc, s0, b, t_ctx, n_heads, BF16)
            ql, gbl, gbtl = _gdn_in(h, gdn_w_in, gdn_conv, jx, ab_par, lat, n_heads)
            o_f, o_b, _ = _gdn_core(ql, gbl, gbtl, s_ctx, b, t, n_heads, BF16)
            mixed = [(_gdn_combine(o_f, o_b, ql, gdn_norm_g[jx], lat.tm), gdn_w_out, None)]
            if ctx_stream:
                mixed.append((_gdn_combine(ocf, ocb, qc, gdn_norm_g[jx], cst.tm), gdn_w_out, None))
        elif kind == 1:
            mixed = [(_sc_in(hs, sc_w_in, sc_conv, jx, st), sc_w_out, None) for st, hs, _, _, _ in streams]
        else:
            mixed = [(_pool_delta(hs, b, sl, gw), pool_w, pool_scale[jx]) for st, hs, _, sl, gw in streams]

        new = []
        for (st, _, ys, _, _), (m_lhs, w_out, scale) in zip(streams, mixed):
            y1, h2 = _mm_out(m_lhs, w_out, jx, ys, modr, i, 2, st, norm2_g[i], (i, 3, 4), BF16, col_scale=scale)
            f = _ffn_in(h2, ffn_w_in, ffn_conv, i, st.proj)
            if last:
                _, hn = _mm_out(f, ffn_w_out, i, y1, modr, i, 5, st, final_g, None, F32, emit_y=False)
                y2 = None
            else:
                y2, hn = _mm_out(f, ffn_w_out, i, y1, modr, i, 5, st, norm1_g[i + 1], (i + 1, 0, 1),
                                 h_dtype(i + 1))
            new.append((y2, hn))
        y, h = new[0]
        if ctx_stream:
            yc, hc = new[1]
        if last:
            out = h
    return out.reshape(b, t, d).astype(x.dtype)
```

```python
import functools

import jax
import jax.numpy as jnp
from jax import lax
from jax.experimental import pallas as pl
from jax.experimental.pallas import tpu as pltpu

F32 = jnp.float32
BF16 = jnp.bfloat16

GRID_W = 64
POOL_WINDOWS = (2, 4, 8, 16)
HEAD_DIM = 128
N_DIRS = 2
NORM_EPS = 1e-6
SCAN_BLOCK = 128
LANES = 128
SUBLANES = 8
BF16_ROWS = 16
HALO = BF16_ROWS
MOD_ROWS = 6
SHIFT1, SCALE1, GATE1, SHIFT2, SCALE2, GATE2 = range(MOD_ROWS)
VMEM_LIMIT_BYTES = 56 * 1024 * 1024
ROW_TILE = 512
IN_ROW_TILE = 1024
COL_TILE = 512
GDN_COL_TILE = 1024
WIDE_COL_TILE = 1024
DOT_ROWS = 128
POOL_COLS = 256
CORE_HEADS = 16


def _params(*sem):
    return pltpu.CompilerParams(dimension_semantics=sem, vmem_limit_bytes=VMEM_LIMIT_BYTES)


def _col_tile(n, cap=None):
    cap = COL_TILE if cap is None else cap
    for t in (cap, 512, 256, 128):
        if t <= cap and n % t == 0:
            return t
    raise ValueError(f"no column tile for {n}")


def _silu(x):
    return x * jax.nn.sigmoid(x)


def _rmsnorm(y, g):
    return y * lax.rsqrt(jnp.mean(y * y, axis=-1, keepdims=True) + NORM_EPS) * g


def _mod_kernel(c_ref, w_ref, b_ref, o_ref):
    s = _silu(c_ref[...]).astype(BF16)
    o_ref[...] = jnp.dot(s, w_ref[...].astype(BF16), preferred_element_type=F32) + b_ref[...]


def _mod_vectors(c, c_ctx, mod_w, mod_b):
    depth, d, d6 = mod_w.shape
    b = c.shape[0]
    r = -(-(b + 1) // BF16_ROWS) * BF16_ROWS
    cc = jnp.zeros((r, d), F32).at[:b].set(c).at[b].set(c_ctx)
    tn = _col_tile(d6)
    out = pl.pallas_call(
        _mod_kernel,
        grid=(depth, d6 // tn),
        in_specs=[pl.BlockSpec((r, d), lambda l, j: (0, 0)),
                  pl.BlockSpec((None, d, tn), lambda l, j: (l, 0, j)),
                  pl.BlockSpec((None, 1, tn), lambda l, j: (l, 0, j))],
        out_specs=pl.BlockSpec((None, r, tn), lambda l, j: (l, 0, j)),
        out_shape=jax.ShapeDtypeStruct((depth, r, d6), F32),
        compiler_params=_params("parallel", "parallel"),
        name="mod",
    )(cc, mod_w, mod_b.reshape(depth, 1, d6))
    return out.reshape(depth, r, MOD_ROWS, d)


class _Stream:
    def __init__(self, n_seq, seq_len, tm, ctx_row=None):
        self.n_seq, self.seq_len, self.tm = n_seq, seq_len, tm
        self.rows = n_seq * seq_len
        assert self.rows % tm == 0 and tm % HALO == 0
        if tm > seq_len:
            assert tm % seq_len == 0 and seq_len & (seq_len - 1) == 0 and ctx_row is not None
        else:
            assert seq_len % tm == 0
        self.tiles_per_seq = max(seq_len // tm, 1)
        self.seq_rows = seq_len if tm > seq_len else None
        self.n_tiles = self.rows // tm
        self.ctx_row = ctx_row

    @property
    def geom(self):
        return dict(tiles_per_seq=self.tiles_per_seq, seq_rows=self.seq_rows)

    def mod_row(self, i):
        return i // self.tiles_per_seq if self.ctx_row is None else self.ctx_row

    def mod_spec(self, layer, d, cols=None):
        if cols is None:
            return pl.BlockSpec((None, None, MOD_ROWS, d), lambda i, j: (layer, self.mod_row(i), 0, 0))
        return pl.BlockSpec((None, None, MOD_ROWS, cols), lambda i, j: (layer, self.mod_row(i), 0, j))


def _ctx_tile(cap, n_seq, seq_len):
    if seq_len >= cap or seq_len & (seq_len - 1):
        return min(cap, seq_len)
    per = max(p for p in range(1, n_seq + 1) if n_seq % p == 0 and p * seq_len <= cap)
    return per * seq_len


def _norm_kernel(y_ref, mod_ref, g_ref, o_ref):
    m = mod_ref[...]
    h = _rmsnorm(y_ref[...], g_ref[...]) * (1.0 + m[SCALE1:SCALE1 + 1]) + m[SHIFT1:SHIFT1 + 1]
    o_ref[...] = h.astype(o_ref.dtype)


def _norm_mod(y, modr, layer, g, st, out_dtype):
    rows, d = y.shape
    return pl.pallas_call(
        _norm_kernel,
        grid=(st.n_tiles, 1),
        in_specs=[pl.BlockSpec((st.tm, d), lambda i, j: (i, 0)),
                  st.mod_spec(layer, d),
                  pl.BlockSpec((1, d), lambda i, j: (0, 0))],
        out_specs=pl.BlockSpec((st.tm, d), lambda i, j: (i, 0)),
        out_shape=jax.ShapeDtypeStruct((rows, d), out_dtype),
        compiler_params=_params("parallel", "arbitrary"),
        name="norm_mod",
    )(y, modr, g.reshape(1, d))


def _window_sum(x, pos, n, unit, w):
    tot = x.shape[0]
    fwd, bwd = w - w // 2, w // 2
    assert fwd & (fwd - 1) == 0 and bwd & (bwd - 1) == 0 and bwd >= 1
    a, s = x, 1
    while s < fwd:
        a = a + jnp.where(pos + s < n, pltpu.roll(a, tot - s * unit, 0), 0.0)
        s *= 2
    b, s = jnp.where(pos >= 1, pltpu.roll(x, unit, 0), 0.0), 1
    while s < bwd:
        b = b + jnp.where(pos - s >= 0, pltpu.roll(b, s * unit, 0), 0.0)
        s *= 2
    return a + b


def _window_sum_strided(x, unit, w):
    assert w & (w - 1) == 0 and unit % SUBLANES == 0
    pad = jnp.zeros(((w // 2) * unit, x.shape[1]), x.dtype)
    s, k = jnp.concatenate([pad, x, pad], axis=0), 1
    while k < w:
        m = s.shape[0] - k * unit
        s = s[0:m] + s[k * unit:k * unit + m]
        k *= 2
    return s[0:x.shape[0]]


def _window_count(pos, n, w):
    return (jnp.minimum(pos + (w - w // 2), n) - jnp.maximum(pos - w // 2, 0)).astype(F32)


def _pool_kernel(h_ref, o_ref, *, tiles_per_group, grid_w, grid_rows):
    group = pl.program_id(1) // tiles_per_group
    t = h_ref.shape[0]
    tok = lax.broadcasted_iota(jnp.int32, (t, 1), 0)
    for gi, w in enumerate(POOL_WINDOWS):
        @pl.when(group == gi)
        def _(w=w):
            x = h_ref[...]
            if grid_rows is None:
                s = _window_sum(x, tok, t, 1, w)
                cnt = _window_count(tok, t, w)
            else:
                shift = grid_w.bit_length() - 1
                col, row = tok & (grid_w - 1), tok >> shift
                s = _window_sum_strided(x, grid_w, w)
                s = _window_sum(s, col, grid_w, 1, w)
                cnt = _window_count(row, grid_rows, w) * _window_count(col, grid_w, w)
            o_ref[...] = (s / cnt - x).astype(o_ref.dtype)


def _pool_delta(h, n_seq, seq_len, grid_w):
    rows, d = h.shape
    group = d // len(POOL_WINDOWS)
    ct = min(POOL_COLS, group)
    assert group % ct == 0
    if grid_w is not None:
        assert grid_w & (grid_w - 1) == 0 and seq_len % grid_w == 0
    kern = functools.partial(_pool_kernel, tiles_per_group=group // ct, grid_w=grid_w,
                             grid_rows=None if grid_w is None else seq_len // grid_w)
    return pl.pallas_call(
        kern,
        grid=(n_seq, d // ct),
        in_specs=[pl.BlockSpec((seq_len, ct), lambda b, j: (b, j))],
        out_specs=pl.BlockSpec((seq_len, ct), lambda b, j: (b, j)),
        out_shape=jax.ShapeDtypeStruct((rows, d), BF16),
        compiler_params=_params("parallel", "parallel"),
        name="pool",
    )(h)


def _halo_specs(st, d):
    per = st.tm // HALO
    last = st.rows // HALO - 1
    return [pl.BlockSpec((st.tm, d), lambda i, j: (i, 0)),
            pl.BlockSpec((HALO, d), lambda i, j: (jnp.maximum(i * per - 1, 0), 0)),
            pl.BlockSpec((HALO, d), lambda i, j: (jnp.minimum((i + 1) * per, last), 0))]


def _w_spec(layer, d, tn, offset=0):
    return pl.BlockSpec((None, d, tn), lambda i, j: (layer, 0, j + offset))


def _fill_lhs(lhs_ref, hm_ref, hp_ref, hn_ref, tiles_per_seq):
    tm = hm_ref.shape[0]
    i = pl.program_id(0) % tiles_per_seq
    r = lax.broadcasted_iota(jnp.int32, hp_ref.shape, 0)
    nxt = jnp.where(i == tiles_per_seq - 1, jnp.zeros_like(hn_ref[...]), hn_ref[...])
    prv = jnp.where(i == 0, jnp.zeros_like(hp_ref[...]), hp_ref[...])
    lhs_ref[0:HALO, :] = jnp.where(r < HALO // 2, nxt, prv)
    lhs_ref[HALO:HALO + tm, :] = hm_ref[...]


def _dot_row_chunks(lhs_ref, w, first=0):
    n = lhs_ref.shape[0]
    bounds = [first] + list(range(first + (n - first) % DOT_ROWS + DOT_ROWS, n, DOT_ROWS)) + [n]
    parts = [jnp.dot(lhs_ref[r0:r1, :], w, preferred_element_type=F32) for r0, r1 in zip(bounds, bounds[1:])]
    return parts[0] if len(parts) == 1 else jnp.concatenate(parts, axis=0)


def _conv3(a_ext, cw, tm, seq_rows):
    n = a_ext.shape[0]
    dn = pltpu.roll(a_ext, 1, 0)[HALO:HALO + tm]
    up = pltpu.roll(a_ext, n - 1, 0)[HALO:HALO + tm]
    if seq_rows is not None:
        pos = lax.broadcasted_iota(jnp.int32, (tm, 1), 0) & (seq_rows - 1)
        dn = jnp.where(pos == 0, 0.0, dn)
        up = jnp.where(pos == seq_rows - 1, 0.0, up)
    return cw[0:1] * dn + cw[1:2] * a_ext[HALO:HALO + tm] + cw[2:3] * up


def _ffn_in_kernel(hm_ref, hp_ref, hn_ref, wa_ref, wu_ref, cw_ref, o_ref, lhs_ref, *, tiles_per_seq, seq_rows):
    tm = hm_ref.shape[0]

    @pl.when(pl.program_id(1) == 0)
    def _():
        _fill_lhs(lhs_ref, hm_ref, hp_ref, hn_ref, tiles_per_seq)

    a = jnp.dot(lhs_ref[...], wa_ref[...], preferred_element_type=F32)
    u = jnp.dot(lhs_ref[HALO:HALO + tm, :], wu_ref[...], preferred_element_type=F32)
    o_ref[...] = (_silu(_conv3(a, cw_ref[...], tm, seq_rows)) * u).astype(o_ref.dtype)


def _ffn_in(h, w_in, conv, layer, st):
    rows, d = h.shape
    f = conv.shape[-1]
    tn = _col_tile(f)
    nj = f // tn
    kern = functools.partial(_ffn_in_kernel, **st.geom)
    return pl.pallas_call(
        kern,
        grid=(st.n_tiles, nj),
        in_specs=_halo_specs(st, d) + [_w_spec(layer, d, tn), _w_spec(layer, d, tn, nj),
                                       pl.BlockSpec((None, 3, tn), lambda i, j: (layer, 0, j))],
        out_specs=pl.BlockSpec((st.tm, tn), lambda i, j: (i, j)),
        out_shape=jax.ShapeDtypeStruct((rows, f), BF16),
        scratch_shapes=[pltpu.VMEM((st.tm + HALO, d), BF16)],
        compiler_params=_params("parallel", "arbitrary"),
        name="ffn_in",
    )(h, h, h, w_in, w_in, conv)


def _sc_in_kernel(hm_ref, hp_ref, hn_ref, wb_ref, wc_ref, wv_ref, cw_ref, o_ref, lhs_ref, *, tiles_per_seq,
                  seq_rows):
    tm = hm_ref.shape[0]

    @pl.when(pl.program_id(1) == 0)
    def _():
        _fill_lhs(lhs_ref, hm_ref, hp_ref, hn_ref, tiles_per_seq)

    lhs = lhs_ref[...]
    cg = jnp.dot(lhs, wc_ref[...], preferred_element_type=F32)
    v = jnp.dot(lhs, wv_ref[...], preferred_element_type=F32)
    bg = jnp.dot(lhs_ref[HALO:HALO + tm, :], wb_ref[...], preferred_element_type=F32)
    o_ref[...] = (bg * _conv3(cg * v, cw_ref[...], tm, seq_rows)).astype(o_ref.dtype)


def _sc_in(h, w_in, conv, layer, st):
    rows, d = h.shape
    n = conv.shape[-1]
    tn = _col_tile(n)
    nj = n // tn
    kern = functools.partial(_sc_in_kernel, **st.geom)
    return pl.pallas_call(
        kern,
        grid=(st.n_tiles, nj),
        in_specs=_halo_specs(st, d) + [_w_spec(layer, d, tn), _w_spec(layer, d, tn, nj),
                                       _w_spec(layer, d, tn, 2 * nj),
                                       pl.BlockSpec((None, 3, tn), lambda i, j: (layer, 0, j))],
        out_specs=pl.BlockSpec((st.tm, tn), lambda i, j: (i, j)),
        out_shape=jax.ShapeDtypeStruct((rows, n), BF16),
        scratch_shapes=[pltpu.VMEM((st.tm + HALO, d), BF16)],
        compiler_params=_params("parallel", "arbitrary"),
        name="sc_in",
    )(h, h, h, w_in, w_in, w_in, conv)


def _block_scan(x, pos, block, reverse):
    n = x.shape[0]
    s = 1
    while s < block:
        if reverse:
            x = x + jnp.where(pos + s < block, pltpu.roll(x, n - s, 0), 0.0)
        else:
            x = x + jnp.where(pos >= s, pltpu.roll(x, s, 0), 0.0)
        s *= 2
    return x


def _gdn_in_kernel(hm_ref, hp_ref, hn_ref, w_ref, wab_ref, cw_ref, ab_ref, o_ref, gb_ref, gbt_ref, lhs_ref,
                   *, tiles_per_seq, seq_rows, n_qk_tiles, n_conv_tiles, n_heads):
    tm = hm_ref.shape[0]
    tn = w_ref.shape[1]
    j = pl.program_id(1)

    @pl.when(j == 0)
    def _():
        _fill_lhs(lhs_ref, hm_ref, hp_ref, hn_ref, tiles_per_seq)
        lane = lax.broadcasted_iota(jnp.int32, (1, LANES), 1)
        ab = jnp.dot(lhs_ref[HALO:HALO + tm, :], wab_ref[...], preferred_element_type=F32)
        ab = jnp.where(lane < 2 * N_DIRS * n_heads, ab, 0.0)
        p = ab_ref[...]
        x = ab + p[1:2]
        log_alpha = -jnp.exp(p[0:1]) * (jnp.maximum(x, 0.0) + jnp.log1p(jnp.exp(-jnp.abs(x))))
        beta = jax.nn.sigmoid(ab)
        pos = lax.broadcasted_iota(jnp.int32, (tm, 1), 0) & (SCAN_BLOCK - 1)
        g = jnp.where(lane < n_heads, _block_scan(log_alpha, pos, SCAN_BLOCK, False),
                      _block_scan(log_alpha, pos, SCAN_BLOCK, True))
        gb = jnp.where(lane < N_DIRS * n_heads, g, beta)
        gb_ref[...] = gb
        gbt_ref[...] = gb.T

    scale = jnp.where(j < n_qk_tiles // 2, HEAD_DIM ** -0.5, 1.0).astype(F32)
    sub = min(tn, COL_TILE)
    starts = range(0, tn, sub)
    parts = [_dot_row_chunks(lhs_ref, w_ref[:, c0:c0 + sub]) for c0 in starts]
    for c0, a in zip(starts, parts):
        x = jnp.where(j < n_conv_tiles, _silu(_conv3(a, cw_ref[:, c0:c0 + sub], tm, seq_rows)), a[HALO:HALO + tm])
        for hh in range(sub // HEAD_DIM):
            xh = x[:, hh * HEAD_DIM:(hh + 1) * HEAD_DIM]
            inv = lax.rsqrt(jnp.sum(xh * xh, axis=-1, keepdims=True) + NORM_EPS) * scale
            o_ref[c0 // HEAD_DIM + hh] = (xh * jnp.where(j < n_qk_tiles, inv, 1.0)).astype(o_ref.dtype)


def _gdn_in(h, w_in, conv, layer, ab_par, st, n_heads):
    rows, d = h.shape
    n = 4 * d
    assert w_in.shape[-1] - n <= LANES
    tn = _col_tile(d, GDN_COL_TILE)
    nj = n // tn
    n_conv = conv.shape[-1] // tn
    kern = functools.partial(_gdn_in_kernel, **st.geom, n_qk_tiles=2 * d // tn,
                             n_conv_tiles=n_conv, n_heads=n_heads)
    hpt = tn // HEAD_DIM
    return pl.pallas_call(
        kern,
        grid=(st.n_tiles, nj),
        in_specs=_halo_specs(st, d) + [
            _w_spec(layer, d, tn),
            pl.BlockSpec((None, d, LANES), lambda i, j: (layer, 0, n // LANES)),
            pl.BlockSpec((None, 3, tn), lambda i, j: (layer, 0, jnp.minimum(j, n_conv - 1))),
            pl.BlockSpec((2, LANES), lambda i, j: (0, 0))],
        out_specs=[pl.BlockSpec((hpt, st.tm, HEAD_DIM), lambda i, j: (j, i, 0)),
                   pl.BlockSpec((st.tm, LANES), lambda i, j: (i, 0)),
                   pl.BlockSpec((LANES, st.tm), lambda i, j: (0, i))],
        out_shape=[jax.ShapeDtypeStruct((n // HEAD_DIM, rows, HEAD_DIM), BF16),
                   jax.ShapeDtypeStruct((rows, LANES), F32),
                   jax.ShapeDtypeStruct((LANES, rows), F32)],
        scratch_shapes=[pltpu.VMEM((st.tm + HALO, d), BF16)],
        compiler_params=_params("parallel", "arbitrary"),
        name="gdn_in",
    )(h, h, h, w_in, w_in, conv, ab_par)


def _each(f, *lists):
    return [f(*xs) for xs in zip(*lists)]


def _bdot(a, b):
    return _each(lambda x, y: jnp.dot(x.astype(BF16), y.astype(BF16), preferred_element_type=F32), a, b)


def _unit_tri_solve(neg_l, rhs, ri, ci, upper):
    c = neg_l[0].shape[0]
    base = SUBLANES

    def same_block(width):
        sh = width.bit_length() - 1
        return (ri >> sh) == (ci >> sh)

    diag = {up: jnp.logical_and(same_block(base), (ri < ci) if up else (ri > ci)) for up in set(upper)}
    n0 = [jnp.where(diag[up], m, 0.0) for m, up in zip(neg_l, upper)]
    p2 = _bdot(n0, n0)
    t = _bdot([jnp.concatenate([a, b], axis=0) for a, b in zip(n0, p2)], p2)
    x = _each(lambda a, b, tt: a + b + tt[0:c], n0, p2, t)
    p4 = [tt[c:2 * c] for tt in t]
    x = _each(lambda xx, p, u: xx + p + u, x, p4, _bdot(x, p4))
    size = base
    while size < c:
        pair = jnp.logical_and(same_block(2 * size), jnp.logical_not(same_block(size)))
        cm = [jnp.where(pair, m, 0.0) for m in neg_l]
        blocks = [range(0 if up else 1, c // size, 2) for up in upper]
        pick = lambda m, bl: jnp.concatenate([m[b * size:(b + 1) * size] for b in bl], axis=0)
        cm_h = _each(pick, cm, blocks)
        a_h = _each(lambda m, u: m + u, cm_h, _bdot(_each(pick, x, blocks), cm))
        a_h = _each(lambda aa, u: aa + u, a_h, _bdot(a_h, x))
        zero = jnp.zeros((size, c), F32)

        def spread(aa, bl):
            slabs = [aa[k * size:(k + 1) * size] for k in range(len(bl))]
            return jnp.concatenate([s for k in range(len(bl)) for s in
                                    ((slabs[k], zero) if bl[0] == 0 else (zero, slabs[k]))], axis=0)

        x = _each(lambda xx, aa, bl: xx + spread(aa, bl), x, a_h, blocks)
        size *= 2
    return _each(lambda r, u: r + u, rhs, _bdot(x, rhs))


def _delta_blocks(q, k, v, gcol, bcol, grow, s, reverse):
    c = q[0].shape[0]
    ri = lax.broadcasted_iota(jnp.int32, (c, c), 0)
    ci = lax.broadcasted_iota(jnp.int32, (c, c), 1)
    masks = {r: (ri <= ci) if r else (ri >= ci) for r in set(reverse)}
    seen = [masks[r] for r in reverse]
    g_end = [g[0:1] if r else g[c - 1:c] for g, r in zip(gcol, reverse)]
    kf = [x.astype(F32) for x in k]
    kbn = _each(lambda x, b: x * (-b), kf, bcol)
    eg = [jnp.exp(g) for g in gcol]
    kq = _each(lambda x, y, z: lax.dot_general(jnp.concatenate([x.astype(BF16), y], axis=0), z,
                                               (((1,), (1,)), ((), ())), preferred_element_type=F32), kbn, q, k)
    gam = _each(lambda m, gc, gr: jnp.exp(jnp.where(m, gc - gr, -jnp.inf)), seen, gcol, grow)
    neg_l = _each(lambda x, gm: x[0:c] * gm, kq, gam)
    qk = _each(lambda x, gm: x[c:2 * c] * gm, kq, gam)
    rhs = _each(lambda x, b, y, e: jnp.concatenate([x.astype(F32) * b, y * (-e)], axis=1), v, bcol, kbn, eg)
    y = _unit_tri_solve(neg_l, rhs, ri, ci, reverse)
    q_dec = _each(lambda x, e: x.astype(F32) * e, q, eg)
    k_dec = _each(lambda x, ge, g: x * jnp.exp(ge - g), kf, g_end, gcol)
    vs = _bdot(_each(lambda yy, qd: jnp.concatenate([yy[:, HEAD_DIM:2 * HEAD_DIM], qd], axis=0), y, q_dec), s)
    v_new = _each(lambda yy, u: (yy[:, 0:HEAD_DIM] - u[0:c]).astype(BF16), y, vs)
    o = _each(lambda u, w: u[c:2 * c] + w, vs, _bdot(qk, v_new))
    kv = _each(lambda x, w: lax.dot_general(x.astype(BF16), w, (((0,), (0,)), ((), ())),
                                            preferred_element_type=F32), k_dec, v_new)
    s_new = _each(lambda ss, ge, u: ss * jnp.exp(ge) + u, s, g_end, kv)
    return o, s_new


def _gdn_core_kernel(qf_ref, kf_ref, vf_ref, qb_ref, kb_ref, vb_ref, gbf_ref, gbb_ref, gtf_ref, gtb_ref,
                     s0_ref, of_ref, ob_ref, sout_ref, s_ref, *, n_heads):
    n = pl.program_id(2)
    heads = qf_ref.shape[0]
    h0 = pl.program_id(1) * heads

    @pl.when(n == 0)
    def _():
        s_ref[...] = s0_ref[...]

    lane = lax.broadcasted_iota(jnp.int32, (1, LANES), 1)
    dirs = ((qf_ref, kf_ref, vf_ref, gbf_ref, gtf_ref, of_ref, False),
            (qb_ref, kb_ref, vb_ref, gbb_ref, gtb_ref, ob_ref, True))
    q, k, v, gcol, bcol, grow, s, reverse, dest = [], [], [], [], [], [], [], [], []
    for d, (q_ref, k_ref, v_ref, gb_ref, gt_ref, o_ref, rev) in enumerate(dirs):
        gb = gb_ref[...]
        for hh in range(heads):
            col = d * n_heads + h0 + hh
            gcol.append(jnp.sum(jnp.where(lane == col, gb, 0.0), axis=1, keepdims=True))
            bcol.append(jnp.sum(jnp.where(lane == col + N_DIRS * n_heads, gb, 0.0), axis=1, keepdims=True))
            grow.append(gt_ref[pl.ds(col, 1), :])
            q.append(q_ref[hh])
            k.append(k_ref[hh])
            v.append(v_ref[hh])
            s.append(s_ref[hh, d])
            reverse.append(rev)
            dest.append((o_ref, hh, d))
    o, s_new = _delta_blocks(q, k, v, gcol, bcol, grow, s, reverse)
    for (o_ref, hh, d), oo, ss in zip(dest, o, s_new):
        o_ref[hh] = oo.astype(o_ref.dtype)
        s_ref[hh, d] = ss

    @pl.when(n == pl.num_programs(2) - 1)
    def _():
        sout_ref[...] = s_ref[...]


def _gdn_core(qkvg, gb, gbt, s0, n_seq, seq_len, n_heads, o_dtype):
    rows = n_seq * seq_len
    c = SCAN_BLOCK
    assert seq_len % c == 0
    nc = seq_len // c
    g = min(CORE_HEADS, n_heads)
    assert n_heads % g == 0
    ng = n_heads // g

    def fwd(b, n):
        return b * nc + n

    def bwd(b, n):
        return b * nc + nc - 1 - n

    def head_spec(part, blk):
        return pl.BlockSpec((g, c, HEAD_DIM), lambda b, hg, n: (part * ng + hg, blk(b, n), 0))

    state_spec = pl.BlockSpec((None, g, N_DIRS, HEAD_DIM, HEAD_DIM), lambda b, hg, n: (b, hg, 0, 0, 0))
    kern = functools.partial(_gdn_core_kernel, n_heads=n_heads)
    return pl.pallas_call(
        kern,
        grid=(n_seq, ng, nc),
        in_specs=[head_spec(0, fwd), head_spec(1, fwd), head_spec(2, fwd),
                  head_spec(0, bwd), head_spec(1, bwd), head_spec(2, bwd),
                  pl.BlockSpec((c, LANES), lambda b, hg, n: (fwd(b, n), 0)),
                  pl.BlockSpec((c, LANES), lambda b, hg, n: (bwd(b, n), 0)),
                  pl.BlockSpec((LANES, c), lambda b, hg, n: (0, fwd(b, n))),
                  pl.BlockSpec((LANES, c), lambda b, hg, n: (0, bwd(b, n))),
                  state_spec],
        out_specs=[pl.BlockSpec((g, c, HEAD_DIM), lambda b, hg, n: (hg, fwd(b, n), 0)),
                   pl.BlockSpec((g, c, HEAD_DIM), lambda b, hg, n: (hg, bwd(b, n), 0)),
                   state_spec],
        out_shape=[jax.ShapeDtypeStruct((n_heads, rows, HEAD_DIM), o_dtype),
                   jax.ShapeDtypeStruct((n_heads, rows, HEAD_DIM), o_dtype),
                   jax.ShapeDtypeStruct(s0.shape, F32)],
        scratch_shapes=[pltpu.VMEM((g, N_DIRS, HEAD_DIM, HEAD_DIM), F32)],
        compiler_params=_params("parallel", "parallel", "arbitrary"),
        name="gdn_core",
    )(qkvg, qkvg, qkvg, qkvg, qkvg, qkvg, gb, gb, gbt, gbt, s0)


def _gdn_combine_kernel(of_ref, ob_ref, gate_ref, g_ref, o_ref):
    for hh in range(of_ref.shape[0]):
        o = of_ref[hh].astype(F32) + ob_ref[hh].astype(F32)
        on = _rmsnorm(o, g_ref[...])
        o_ref[:, hh * HEAD_DIM:(hh + 1) * HEAD_DIM] = (on * _silu(gate_ref[hh].astype(F32))).astype(o_ref.dtype)


def _gdn_combine(o_f, o_b, qkvg, norm_g, tm):
    n_heads, rows, _ = o_f.shape
    g = min(CORE_HEADS, n_heads)
    ng = n_heads // g
    return pl.pallas_call(
        _gdn_combine_kernel,
        grid=(rows // tm, ng),
        in_specs=[pl.BlockSpec((g, tm, HEAD_DIM), lambda i, j: (j, i, 0)),
                  pl.BlockSpec((g, tm, HEAD_DIM), lambda i, j: (j, i, 0)),
                  pl.BlockSpec((g, tm, HEAD_DIM), lambda i, j: (3 * ng + j, i, 0)),
                  pl.BlockSpec((1, HEAD_DIM), lambda i, j: (0, 0))],
        out_specs=pl.BlockSpec((tm, g * HEAD_DIM), lambda i, j: (i, j)),
        out_shape=jax.ShapeDtypeStruct((rows, n_heads * HEAD_DIM), BF16),
        compiler_params=_params("parallel", "parallel"),
        name="gdn_combine",
    )(o_f, o_b, qkvg, norm_g.reshape(1, HEAD_DIM))


def _mm_out_kernel(*refs, gate_row, next_rows, has_scale, emit_y):
    it = iter(refs)
    lhs_ref, w_ref, y_ref, gate_ref = next(it), next(it), next(it), next(it)
    scale_ref = next(it) if has_scale else None
    gn_ref = next(it)
    modn_ref = next(it) if next_rows is not None else None
    ynew_ref = next(it) if emit_y else None
    hn_ref, row_ref = next(it), next(it)
    j = pl.program_id(1)
    nj, _, tn = row_ref.shape

    acc = jnp.dot(lhs_ref[...], w_ref[...], preferred_element_type=F32)
    if has_scale:
        acc = acc * scale_ref[...]
    y_new = y_ref[...] + gate_ref[gate_row:gate_row + 1, :] * acc
    if emit_y:
        ynew_ref[...] = y_new
    row_ref[j] = y_new

    @pl.when(j == nj - 1)
    def _():
        ss = jnp.zeros((row_ref.shape[1], 1), F32)
        for jj in range(nj):
            yj = row_ref[jj]
            ss = ss + jnp.sum(yj * yj, axis=-1, keepdims=True)
        inv = lax.rsqrt(ss / (nj * tn) + NORM_EPS)
        for jj in range(nj):
            cols = slice(jj * tn, (jj + 1) * tn)
            hj = row_ref[jj] * inv * gn_ref[:, cols]
            if next_rows is not None:
                sh, sc = next_rows
                hj = hj * (1.0 + modn_ref[sc:sc + 1, cols]) + modn_ref[sh:sh + 1, cols]
            hn_ref[:, cols] = hj.astype(hn_ref.dtype)


def _mm_out(lhs, w, w_layer, y, modr, layer, gate_row, st, next_g, next_mod, out_dtype, col_scale=None,
            emit_y=True):
    rows, n = y.shape
    grouped = w.ndim == 4
    k = w.shape[-2]
    group_n = w.shape[-1]
    tn = _col_tile(group_n, WIDE_COL_TILE if grouped else None)
    nj = n // tn
    per_group = group_n // tn
    if grouped:
        lhs_spec = pl.BlockSpec((st.tm, k), lambda i, j: (i, j // per_group))
        w_spec = pl.BlockSpec((None, None, k, tn), lambda i, j: (w_layer, j // per_group, 0, j % per_group))
    else:
        lhs_spec = pl.BlockSpec((st.tm, k), lambda i, j: (i, 0))
        w_spec = pl.BlockSpec((None, k, tn), lambda i, j: (w_layer, 0, j))
    in_specs = [lhs_spec, w_spec,
                pl.BlockSpec((st.tm, tn), lambda i, j: (i, j)),
                st.mod_spec(layer, n, tn)]
    args = [lhs, w, y, modr]
    if col_scale is not None:
        in_specs.append(pl.BlockSpec((1, tn), lambda i, j: (0, j)))
        args.append(col_scale.reshape(1, n))
    in_specs.append(pl.BlockSpec((1, n), lambda i, j: (0, 0)))
    args.append(next_g.reshape(1, n))
    next_rows = None
    if next_mod is not None:
        in_specs.append(st.mod_spec(next_mod[0], n))
        args.append(modr)
        next_rows = (next_mod[1], next_mod[2])
    out_specs, out_shape = [], []
    if emit_y:
        out_specs.append(pl.BlockSpec((st.tm, tn), lambda i, j: (i, j)))
        out_shape.append(jax.ShapeDtypeStruct((rows, n), F32))
    out_specs.append(pl.BlockSpec((st.tm, n), lambda i, j: (i, 0)))
    out_shape.append(jax.ShapeDtypeStruct((rows, n), out_dtype))
    kern = functools.partial(_mm_out_kernel, gate_row=gate_row, next_rows=next_rows,
                             has_scale=col_scale is not None, emit_y=emit_y)
    res = pl.pallas_call(
        kern,
        grid=(st.n_tiles, nj),
        in_specs=in_specs,
        out_specs=out_specs,
        out_shape=out_shape,
        scratch_shapes=[pltpu.VMEM((nj, st.tm, tn), F32)],
        compiler_params=_params("parallel", "arbitrary"),
        name="mm_out",
    )(*args)
    return res if emit_y else (None, res[0])


def kernel(x, c, ctx, c_ctx, mod_w, mod_b, norm1_g, norm2_g, ffn_w_in, ffn_conv, ffn_w_out, pool_w, pool_scale, sc_w_in, sc_conv, sc_w_out, gdn_w_in, gdn_conv, gdn_a_log, gdn_dt_bias, gdn_norm_g, gdn_w_out, final_g):
    b, t, d = x.shape
    t_ctx = ctx.shape[1]
    depth = mod_w.shape[0]
    n_mix = 3
    n_heads = d // HEAD_DIM
    assert 2 * N_DIRS * n_heads <= LANES
    gdn_layers = [i for i in range(depth) if i % n_mix == 2]
    last_ctx = gdn_layers[-1] if gdn_layers else -1

    lat = _Stream(b, t, min(ROW_TILE, t))
    cst = _Stream(b, t_ctx, _ctx_tile(ROW_TILE, b, t_ctx), ctx_row=b)
    lat.proj = _Stream(b, t, min(IN_ROW_TILE, t))
    cst.proj = _Stream(b, t_ctx, _ctx_tile(IN_ROW_TILE, b, t_ctx), ctx_row=b)
    modr = _mod_vectors(c, c_ctx, mod_w, mod_b)

    y = x.reshape(b * t, d)
    yc = ctx.reshape(b * t_ctx, d)

    def h_dtype(i):
        return F32 if i % n_mix == 0 else BF16

    h = _norm_mod(y, modr, 0, norm1_g[0], lat, h_dtype(0))
    hc = _norm_mod(yc, modr, 0, norm1_g[0], cst, h_dtype(0)) if last_ctx >= 0 else None

    ffn_w_in, ffn_w_out, pool_w, sc_w_in, sc_w_out, gdn_w_in, gdn_w_out = (
        w.astype(BF16) for w in (ffn_w_in, ffn_w_out, pool_w, sc_w_in, sc_w_out, gdn_w_in, gdn_w_out))

    out = None
    for i in range(depth):
        kind, jx = i % n_mix, i // n_mix
        ctx_stream = i < last_ctx
        last = i == depth - 1
        streams = [(lat, h, y, t, GRID_W)]
        if ctx_stream:
            streams.append((cst, hc, yc, t_ctx, None))

        if kind == 2:
            ab_par = jnp.pad(jnp.stack([gdn_a_log[jx].reshape(-1), gdn_dt_bias[jx].reshape(-1)]),
                             ((0, 0), (0, LANES - N_DIRS * n_heads)))
            qc, gbc, gbtc = _gdn_in(hc, gdn_w_in, gdn_conv, jx, ab_par, cst, n_heads)
            s0 = jnp.zeros((b, n_heads, N_DIRS, HEAD_DIM, HEAD_DIM), F32)
            ocf, ocb, s_ctx = _gdn_core(qc, gbc, gbtc, s0, b, t_ctx, n_heads, BF16)
            ql, gbl, gbtl = _gdn_in(h, gdn_w_in, gdn_conv, jx, ab_par, lat, n_heads)
            o_f, o_b, _ = _gdn_core(ql, gbl, gbtl, s_ctx, b, t, n_heads, BF16)
            mixed = [(_gdn_combine(o_f, o_b, ql, gdn_norm_g[jx], lat.tm), gdn_w_out, None)]
            if ctx_stream:
                mixed.append((_gdn_combine(ocf, ocb, qc, gdn_norm_g[jx], cst.tm), gdn_w_out, None))
        elif kind == 1:
            mixed = [(_sc_in(hs, sc_w_in, sc_conv, jx, st), sc_w_out, None) for st, hs, _, _, _ in streams]
        else:
            mixed = [(_pool_delta(hs, b, sl, gw), pool_w, pool_scale[jx]) for st, hs, _, sl, gw in streams]

        new = []
        for (st, _, ys, _, _), (m_lhs, w_out, scale) in zip(streams, mixed):
            y1, h2 = _mm_out(m_lhs, w_out, jx, ys, modr, i, GATE1, st, norm2_g[i], (i, SHIFT2, SCALE2), BF16,
                             col_scale=scale)
            f = _ffn_in(h2, ffn_w_in, ffn_conv, i, st.proj)
            if last:
                _, hn = _mm_out(f, ffn_w_out, i, y1, modr, i, GATE2, st, final_g, None, F32, emit_y=False)
                y2 = None
            else:
                y2, hn = _mm_out(f, ffn_w_out, i, y1, modr, i, GATE2, st, norm1_g[i + 1],
                                 (i + 1, SHIFT1, SCALE1), h_dtype(i + 1))
            new.append((y2, hn))
        y, h = new[0]
        if ctx_stream:
            yc, hc = new[1]
        if last:
            out = h
    return out.reshape(b, t, d).astype(x.dtype)
```

```python
import functools

import jax
import jax.numpy as jnp
from jax import lax
from jax.experimental import pallas as pl
from jax.experimental.pallas import tpu as pltpu

F32 = jnp.float32
BF16 = jnp.bfloat16

GRID_W = 64
POOL_WINDOWS = (2, 4, 8, 16)
HEAD_DIM = 128
N_DIRS = 2
NORM_EPS = 1e-6
SCAN_BLOCK = 128
LANES = 128
SUBLANES = 8
BF16_ROWS = 16
HALO = BF16_ROWS
MOD_ROWS = 6
SHIFT1, SCALE1, GATE1, SHIFT2, SCALE2, GATE2 = range(MOD_ROWS)
VMEM_LIMIT_BYTES = 56 * 1024 * 1024
ROW_TILE = 512
IN_ROW_TILE = 1024
COL_TILE = 512
GDN_COL_TILE = 512
WIDE_COL_TILE = 1024
DOT_ROWS = 128
POOL_COLS = 256
CORE_HEADS = 16


def _params(*sem):
    return pltpu.CompilerParams(dimension_semantics=sem, vmem_limit_bytes=VMEM_LIMIT_BYTES)


def _col_tile(n, cap=None):
    cap = COL_TILE if cap is None else cap
    for t in (cap, 512, 256, 128):
        if t <= cap and n % t == 0:
            return t
    raise ValueError(f"no column tile for {n}")


def _silu(x):
    return x * jax.nn.sigmoid(x)


def _rmsnorm(y, g):
    return y * lax.rsqrt(jnp.mean(y * y, axis=-1, keepdims=True) + NORM_EPS) * g


def _mod_kernel(c_ref, w_ref, b_ref, o_ref):
    s = _silu(c_ref[...]).astype(BF16)
    o_ref[...] = jnp.dot(s, w_ref[...].astype(BF16), preferred_element_type=F32) + b_ref[...]


def _mod_vectors(c, c_ctx, mod_w, mod_b):
    depth, d, d6 = mod_w.shape
    b = c.shape[0]
    r = -(-(b + 1) // BF16_ROWS) * BF16_ROWS
    cc = jnp.zeros((r, d), F32).at[:b].set(c).at[b].set(c_ctx)
    tn = _col_tile(d6)
    out = pl.pallas_call(
        _mod_kernel,
        grid=(depth, d6 // tn),
        in_specs=[pl.BlockSpec((r, d), lambda l, j: (0, 0)),
                  pl.BlockSpec((None, d, tn), lambda l, j: (l, 0, j)),
                  pl.BlockSpec((None, 1, tn), lambda l, j: (l, 0, j))],
        out_specs=pl.BlockSpec((None, r, tn), lambda l, j: (l, 0, j)),
        out_shape=jax.ShapeDtypeStruct((depth, r, d6), F32),
        compiler_params=_params("parallel", "parallel"),
        name="mod",
    )(cc, mod_w, mod_b.reshape(depth, 1, d6))
    return out.reshape(depth, r, MOD_ROWS, d)


class _Stream:
    def __init__(self, n_seq, seq_len, tm, ctx_row=None):
        self.n_seq, self.seq_len, self.tm = n_seq, seq_len, tm
        self.rows = n_seq * seq_len
        assert self.rows % tm == 0 and tm % HALO == 0
        if tm > seq_len:
            assert tm % seq_len == 0 and seq_len & (seq_len - 1) == 0 and ctx_row is not None
        else:
            assert seq_len % tm == 0
        self.tiles_per_seq = max(seq_len // tm, 1)
        self.seq_rows = seq_len if tm > seq_len else None
        self.n_tiles = self.rows // tm
        self.ctx_row = ctx_row

    @property
    def geom(self):
        return dict(tiles_per_seq=self.tiles_per_seq, seq_rows=self.seq_rows)

    def mod_row(self, i):
        return i // self.tiles_per_seq if self.ctx_row is None else self.ctx_row

    def mod_spec(self, layer, d, cols=None):
        if cols is None:
            return pl.BlockSpec((None, None, MOD_ROWS, d), lambda i, j: (layer, self.mod_row(i), 0, 0))
        return pl.BlockSpec((None, None, MOD_ROWS, cols), lambda i, j: (layer, self.mod_row(i), 0, j))


def _ctx_tile(cap, n_seq, seq_len):
    if seq_len >= cap or seq_len & (seq_len - 1):
        return min(cap, seq_len)
    per = max(p for p in range(1, n_seq + 1) if n_seq % p == 0 and p * seq_len <= cap)
    return per * seq_len


def _norm_kernel(y_ref, mod_ref, g_ref, o_ref):
    m = mod_ref[...]
    h = _rmsnorm(y_ref[...], g_ref[...]) * (1.0 + m[SCALE1:SCALE1 + 1]) + m[SHIFT1:SHIFT1 + 1]
    o_ref[...] = h.astype(o_ref.dtype)


def _norm_mod(y, modr, layer, g, st, out_dtype):
    rows, d = y.shape
    return pl.pallas_call(
        _norm_kernel,
        grid=(st.n_tiles, 1),
        in_specs=[pl.BlockSpec((st.tm, d), lambda i, j: (i, 0)),
                  st.mod_spec(layer, d),
                  pl.BlockSpec((1, d), lambda i, j: (0, 0))],
        out_specs=pl.BlockSpec((st.tm, d), lambda i, j: (i, 0)),
        out_shape=jax.ShapeDtypeStruct((rows, d), out_dtype),
        compiler_params=_params("parallel", "arbitrary"),
        name="norm_mod",
    )(y, modr, g.reshape(1, d))


def _window_sum(x, pos, n, unit, w):
    tot = x.shape[0]
    fwd, bwd = w - w // 2, w // 2
    assert fwd & (fwd - 1) == 0 and bwd & (bwd - 1) == 0 and bwd >= 1
    a, s = x, 1
    while s < fwd:
        a = a + jnp.where(pos + s < n, pltpu.roll(a, tot - s * unit, 0), 0.0)
        s *= 2
    b, s = jnp.where(pos >= 1, pltpu.roll(x, unit, 0), 0.0), 1
    while s < bwd:
        b = b + jnp.where(pos - s >= 0, pltpu.roll(b, s * unit, 0), 0.0)
        s *= 2
    return a + b


def _window_sum_strided(x, unit, w):
    assert w & (w - 1) == 0 and unit % SUBLANES == 0
    pad = jnp.zeros(((w // 2) * unit, x.shape[1]), x.dtype)
    s, k = jnp.concatenate([pad, x, pad], axis=0), 1
    while k < w:
        m = s.shape[0] - k * unit
        s = s[0:m] + s[k * unit:k * unit + m]
        k *= 2
    return s[0:x.shape[0]]


def _window_count(pos, n, w):
    return (jnp.minimum(pos + (w - w // 2), n) - jnp.maximum(pos - w // 2, 0)).astype(F32)


def _pool_kernel(h_ref, o_ref, *, tiles_per_group, grid_w, grid_rows):
    group = pl.program_id(1) // tiles_per_group
    t = h_ref.shape[0]
    tok = lax.broadcasted_iota(jnp.int32, (t, 1), 0)
    for gi, w in enumerate(POOL_WINDOWS):
        @pl.when(group == gi)
        def _(w=w):
            x = h_ref[...]
            if grid_rows is None:
                s = _window_sum(x, tok, t, 1, w)
                cnt = _window_count(tok, t, w)
            else:
                shift = grid_w.bit_length() - 1
                col, row = tok & (grid_w - 1), tok >> shift
                s = _window_sum_strided(x, grid_w, w)
                s = _window_sum(s, col, grid_w, 1, w)
                cnt = _window_count(row, grid_rows, w) * _window_count(col, grid_w, w)
            o_ref[...] = (s / cnt - x).astype(o_ref.dtype)


def _pool_delta(h, n_seq, seq_len, grid_w):
    rows, d = h.shape
    group = d // len(POOL_WINDOWS)
    ct = min(POOL_COLS, group)
    assert group % ct == 0
    if grid_w is not None:
        assert grid_w & (grid_w - 1) == 0 and seq_len % grid_w == 0
    kern = functools.partial(_pool_kernel, tiles_per_group=group // ct, grid_w=grid_w,
                             grid_rows=None if grid_w is None else seq_len // grid_w)
    return pl.pallas_call(
        kern,
        grid=(n_seq, d // ct),
        in_specs=[pl.BlockSpec((seq_len, ct), lambda b, j: (b, j))],
        out_specs=pl.BlockSpec((seq_len, ct), lambda b, j: (b, j)),
        out_shape=jax.ShapeDtypeStruct((rows, d), BF16),
        compiler_params=_params("parallel", "parallel"),
        name="pool",
    )(h)


def _halo_specs(st, d):
    per = st.tm // HALO
    last = st.rows // HALO - 1
    return [pl.BlockSpec((st.tm, d), lambda i, j: (i, 0)),
            pl.BlockSpec((HALO, d), lambda i, j: (jnp.maximum(i * per - 1, 0), 0)),
            pl.BlockSpec((HALO, d), lambda i, j: (jnp.minimum((i + 1) * per, last), 0))]


def _w_spec(layer, d, tn, offset=0):
    return pl.BlockSpec((None, d, tn), lambda i, j: (layer, 0, j + offset))


def _fill_lhs(lhs_ref, hm_ref, hp_ref, hn_ref, tiles_per_seq):
    tm = hm_ref.shape[0]
    i = pl.program_id(0) % tiles_per_seq
    r = lax.broadcasted_iota(jnp.int32, hp_ref.shape, 0)
    nxt = jnp.where(i == tiles_per_seq - 1, jnp.zeros_like(hn_ref[...]), hn_ref[...])
    prv = jnp.where(i == 0, jnp.zeros_like(hp_ref[...]), hp_ref[...])
    lhs_ref[0:HALO, :] = jnp.where(r < HALO // 2, nxt, prv)
    lhs_ref[HALO:HALO + tm, :] = hm_ref[...]


def _dot_row_chunks(lhs_ref, w, first=0):
    n = lhs_ref.shape[0]
    bounds = [first] + list(range(first + (n - first) % DOT_ROWS + DOT_ROWS, n, DOT_ROWS)) + [n]
    parts = [jnp.dot(lhs_ref[r0:r1, :], w, preferred_element_type=F32) for r0, r1 in zip(bounds, bounds[1:])]
    return parts[0] if len(parts) == 1 else jnp.concatenate(parts, axis=0)


def _conv3(a_ext, cw, tm, seq_rows):
    n = a_ext.shape[0]
    dn = pltpu.roll(a_ext, 1, 0)[HALO:HALO + tm]
    up = pltpu.roll(a_ext, n - 1, 0)[HALO:HALO + tm]
    if seq_rows is not None:
        pos = lax.broadcasted_iota(jnp.int32, (tm, 1), 0) & (seq_rows - 1)
        dn = jnp.where(pos == 0, 0.0, dn)
        up = jnp.where(pos == seq_rows - 1, 0.0, up)
    return cw[0:1] * dn + cw[1:2] * a_ext[HALO:HALO + tm] + cw[2:3] * up


def _ffn_in_kernel(hm_ref, hp_ref, hn_ref, wa_ref, wu_ref, cw_ref, o_ref, lhs_ref, *, tiles_per_seq, seq_rows):
    tm = hm_ref.shape[0]

    @pl.when(pl.program_id(1) == 0)
    def _():
        _fill_lhs(lhs_ref, hm_ref, hp_ref, hn_ref, tiles_per_seq)

    a = jnp.dot(lhs_ref[...], wa_ref[...], preferred_element_type=F32)
    u = jnp.dot(lhs_ref[HALO:HALO + tm, :], wu_ref[...], preferred_element_type=F32)
    o_ref[...] = (_silu(_conv3(a, cw_ref[...], tm, seq_rows)) * u).astype(o_ref.dtype)


def _ffn_in(h, w_in, conv, layer, st):
    rows, d = h.shape
    f = conv.shape[-1]
    tn = _col_tile(f)
    nj = f // tn
    kern = functools.partial(_ffn_in_kernel, **st.geom)
    return pl.pallas_call(
        kern,
        grid=(st.n_tiles, nj),
        in_specs=_halo_specs(st, d) + [_w_spec(layer, d, tn), _w_spec(layer, d, tn, nj),
                                       pl.BlockSpec((None, 3, tn), lambda i, j: (layer, 0, j))],
        out_specs=pl.BlockSpec((st.tm, tn), lambda i, j: (i, j)),
        out_shape=jax.ShapeDtypeStruct((rows, f), BF16),
        scratch_shapes=[pltpu.VMEM((st.tm + HALO, d), BF16)],
        compiler_params=_params("parallel", "arbitrary"),
        name="ffn_in",
    )(h, h, h, w_in, w_in, conv)


def _sc_in_kernel(hm_ref, hp_ref, hn_ref, wb_ref, wc_ref, wv_ref, cw_ref, o_ref, lhs_ref, *, tiles_per_seq,
                  seq_rows):
    tm = hm_ref.shape[0]

    @pl.when(pl.program_id(1) == 0)
    def _():
        _fill_lhs(lhs_ref, hm_ref, hp_ref, hn_ref, tiles_per_seq)

    lhs = lhs_ref[...]
    cg = jnp.dot(lhs, wc_ref[...], preferred_element_type=F32)
    v = jnp.dot(lhs, wv_ref[...], preferred_element_type=F32)
    bg = jnp.dot(lhs_ref[HALO:HALO + tm, :], wb_ref[...], preferred_element_type=F32)
    o_ref[...] = (bg * _conv3(cg * v, cw_ref[...], tm, seq_rows)).astype(o_ref.dtype)


def _sc_in(h, w_in, conv, layer, st):
    rows, d = h.shape
    n = conv.shape[-1]
    tn = _col_tile(n)
    nj = n // tn
    kern = functools.partial(_sc_in_kernel, **st.geom)
    return pl.pallas_call(
        kern,
        grid=(st.n_tiles, nj),
        in_specs=_halo_specs(st, d) + [_w_spec(layer, d, tn), _w_spec(layer, d, tn, nj),
                                       _w_spec(layer, d, tn, 2 * nj),
                                       pl.BlockSpec((None, 3, tn), lambda i, j: (layer, 0, j))],
        out_specs=pl.BlockSpec((st.tm, tn), lambda i, j: (i, j)),
        out_shape=jax.ShapeDtypeStruct((rows, n), BF16),
        scratch_shapes=[pltpu.VMEM((st.tm + HALO, d), BF16)],
        compiler_params=_params("parallel", "arbitrary"),
        name="sc_in",
    )(h, h, h, w_in, w_in, w_in, conv)


def _block_scan(x, pos, block, reverse):
    n = x.shape[0]
    s = 1
    while s < block:
        if reverse:
            x = x + jnp.where(pos + s < block, pltpu.roll(x, n - s, 0), 0.0)
        else:
            x = x + jnp.where(pos >= s, pltpu.roll(x, s, 0), 0.0)
        s *= 2
    return x


def _gdn_in_kernel(hm_ref, hp_ref, hn_ref, w_ref, wab_ref, cw_ref, ab_ref, o_ref, gb_ref, gbt_ref, lhs_ref,
                   *, tiles_per_seq, seq_rows, n_qk_tiles, n_conv_tiles, n_heads):
    tm = hm_ref.shape[0]
    tn = w_ref.shape[1]
    j = pl.program_id(1)

    @pl.when(j == 0)
    def _():
        _fill_lhs(lhs_ref, hm_ref, hp_ref, hn_ref, tiles_per_seq)
        lane = lax.broadcasted_iota(jnp.int32, (1, LANES), 1)
        ab = jnp.dot(lhs_ref[HALO:HALO + tm, :], wab_ref[...], preferred_element_type=F32)
        ab = jnp.where(lane < 2 * N_DIRS * n_heads, ab, 0.0)
        p = ab_ref[...]
        x = ab + p[1:2]
        log_alpha = -jnp.exp(p[0:1]) * (jnp.maximum(x, 0.0) + jnp.log1p(jnp.exp(-jnp.abs(x))))
        beta = jax.nn.sigmoid(ab)
        pos = lax.broadcasted_iota(jnp.int32, (tm, 1), 0) & (SCAN_BLOCK - 1)
        g = jnp.where(lane < n_heads, _block_scan(log_alpha, pos, SCAN_BLOCK, False),
                      _block_scan(log_alpha, pos, SCAN_BLOCK, True))
        gb = jnp.where(lane < N_DIRS * n_heads, g, beta)
        gb_ref[...] = gb
        gbt_ref[...] = gb.T

    scale = jnp.where(j < n_qk_tiles // 2, HEAD_DIM ** -0.5, 1.0).astype(F32)
    sub = min(tn, COL_TILE)
    starts = range(0, tn, sub)
    parts = [_dot_row_chunks(lhs_ref, w_ref[:, c0:c0 + sub]) for c0 in starts]
    for c0, a in zip(starts, parts):
        x = jnp.where(j < n_conv_tiles, _silu(_conv3(a, cw_ref[:, c0:c0 + sub], tm, seq_rows)), a[HALO:HALO + tm])
        for hh in range(sub // HEAD_DIM):
            xh = x[:, hh * HEAD_DIM:(hh + 1) * HEAD_DIM]
            inv = lax.rsqrt(jnp.sum(xh * xh, axis=-1, keepdims=True) + NORM_EPS) * scale
            o_ref[c0 // HEAD_DIM + hh] = (xh * jnp.where(j < n_qk_tiles, inv, 1.0)).astype(o_ref.dtype)


def _gdn_in(h, w_in, conv, layer, ab_par, st, n_heads):
    rows, d = h.shape
    n = 4 * d
    assert w_in.shape[-1] - n <= LANES
    tn = _col_tile(d, GDN_COL_TILE)
    nj = n // tn
    n_conv = conv.shape[-1] // tn
    kern = functools.partial(_gdn_in_kernel, **st.geom, n_qk_tiles=2 * d // tn,
                             n_conv_tiles=n_conv, n_heads=n_heads)
    hpt = tn // HEAD_DIM
    return pl.pallas_call(
        kern,
        grid=(st.n_tiles, nj),
        in_specs=_halo_specs(st, d) + [
            _w_spec(layer, d, tn),
            pl.BlockSpec((None, d, LANES), lambda i, j: (layer, 0, n // LANES)),
            pl.BlockSpec((None, 3, tn), lambda i, j: (layer, 0, jnp.minimum(j, n_conv - 1))),
            pl.BlockSpec((2, LANES), lambda i, j: (0, 0))],
        out_specs=[pl.BlockSpec((hpt, st.tm, HEAD_DIM), lambda i, j: (j, i, 0)),
                   pl.BlockSpec((st.tm, LANES), lambda i, j: (i, 0)),
                   pl.BlockSpec((LANES, st.tm), lambda i, j: (0, i))],
        out_shape=[jax.ShapeDtypeStruct((n // HEAD_DIM, rows, HEAD_DIM), BF16),
                   jax.ShapeDtypeStruct((rows, LANES), F32),
                   jax.ShapeDtypeStruct((LANES, rows), F32)],
        scratch_shapes=[pltpu.VMEM((st.tm + HALO, d), BF16)],
        compiler_params=_params("parallel", "arbitrary"),
        name="gdn_in",
    )(h, h, h, w_in, w_in, conv, ab_par)


def _each(f, *lists):
    return [f(*xs) for xs in zip(*lists)]


def _bdot(a, b):
    return _each(lambda x, y: jnp.dot(x.astype(BF16), y.astype(BF16), preferred_element_type=F32), a, b)


def _unit_tri_solve(neg_l, rhs, ri, ci, upper):
    c = neg_l[0].shape[0]
    base = SUBLANES

    def same_block(width):
        sh = width.bit_length() - 1
        return (ri >> sh) == (ci >> sh)

    diag = {up: jnp.logical_and(same_block(base), (ri < ci) if up else (ri > ci)) for up in set(upper)}
    n0 = [jnp.where(diag[up], m, 0.0) for m, up in zip(neg_l, upper)]
    p2 = _bdot(n0, n0)
    t = _bdot([jnp.concatenate([a, b], axis=0) for a, b in zip(n0, p2)], p2)
    x = _each(lambda a, b, tt: a + b + tt[0:c], n0, p2, t)
    p4 = [tt[c:2 * c] for tt in t]
    x = _each(lambda xx, p, u: xx + p + u, x, p4, _bdot(x, p4))
    size = base
    while size < c:
        pair = jnp.logical_and(same_block(2 * size), jnp.logical_not(same_block(size)))
        cm = [jnp.where(pair, m, 0.0) for m in neg_l]
        blocks = [range(0 if up else 1, c // size, 2) for up in upper]
        pick = lambda m, bl: jnp.concatenate([m[b * size:(b + 1) * size] for b in bl], axis=0)
        cm_h = _each(pick, cm, blocks)
        a_h = _each(lambda m, u: m + u, cm_h, _bdot(_each(pick, x, blocks), cm))
        a_h = _each(lambda aa, u: aa + u, a_h, _bdot(a_h, x))
        zero = jnp.zeros((size, c), F32)

        def spread(aa, bl):
            slabs = [aa[k * size:(k + 1) * size] for k in range(len(bl))]
            return jnp.concatenate([s for k in range(len(bl)) for s in
                                    ((slabs[k], zero) if bl[0] == 0 else (zero, slabs[k]))], axis=0)

        x = _each(lambda xx, aa, bl: xx + spread(aa, bl), x, a_h, blocks)
        size *= 2
    return _each(lambda r, u: r + u, rhs, _bdot(x, rhs))


def _delta_blocks(q, k, v, gcol, bcol, grow, s, reverse):
    c = q[0].shape[0]
    ri = lax.broadcasted_iota(jnp.int32, (c, c), 0)
    ci = lax.broadcasted_iota(jnp.int32, (c, c), 1)
    masks = {r: (ri <= ci) if r else (ri >= ci) for r in set(reverse)}
    seen = [masks[r] for r in reverse]
    g_end = [g[0:1] if r else g[c - 1:c] for g, r in zip(gcol, reverse)]
    kf = [x.astype(F32) for x in k]
    kbn = _each(lambda x, b: x * (-b), kf, bcol)
    eg = [jnp.exp(g) for g in gcol]
    kq = _each(lambda x, y, z: lax.dot_general(jnp.concatenate([x.astype(BF16), y], axis=0), z,
                                               (((1,), (1,)), ((), ())), preferred_element_type=F32), kbn, q, k)
    gam = _each(lambda m, gc, gr: jnp.exp(jnp.where(m, gc - gr, -jnp.inf)), seen, gcol, grow)
    neg_l = _each(lambda x, gm: x[0:c] * gm, kq, gam)
    qk = _each(lambda x, gm: x[c:2 * c] * gm, kq, gam)
    rhs = _each(lambda x, b, y, e: jnp.concatenate([x.astype(F32) * b, y * (-e)], axis=1), v, bcol, kbn, eg)
    y = _unit_tri_solve(neg_l, rhs, ri, ci, reverse)
    q_dec = _each(lambda x, e: x.astype(F32) * e, q, eg)
    k_dec = _each(lambda x, ge, g: x * jnp.exp(ge - g), kf, g_end, gcol)
    vs = _bdot(_each(lambda yy, qd: jnp.concatenate([yy[:, HEAD_DIM:2 * HEAD_DIM], qd], axis=0), y, q_dec), s)
    v_new = _each(lambda yy, u: (yy[:, 0:HEAD_DIM] - u[0:c]).astype(BF16), y, vs)
    o = _each(lambda u, w: u[c:2 * c] + w, vs, _bdot(qk, v_new))
    kv = _each(lambda x, w: lax.dot_general(x.astype(BF16), w, (((0,), (0,)), ((), ())),
                                            preferred_element_type=F32), k_dec, v_new)
    s_new = _each(lambda ss, ge, u: ss * jnp.exp(ge) + u, s, g_end, kv)
    return o, s_new


def _gdn_core_kernel(qf_ref, kf_ref, vf_ref, qb_ref, kb_ref, vb_ref, gbf_ref, gbb_ref, gtf_ref, gtb_ref,
                     s0_ref, of_ref, ob_ref, sout_ref, s_ref, *, n_heads):
    n = pl.program_id(2)
    heads = qf_ref.shape[0]
    h0 = pl.program_id(1) * heads

    @pl.when(n == 0)
    def _():
        s_ref[...] = s0_ref[...]

    lane = lax.broadcasted_iota(jnp.int32, (1, LANES), 1)
    dirs = ((qf_ref, kf_ref, vf_ref, gbf_ref, gtf_ref, of_ref, False),
            (qb_ref, kb_ref, vb_ref, gbb_ref, gtb_ref, ob_ref, True))
    q, k, v, gcol, bcol, grow, s, reverse, dest = [], [], [], [], [], [], [], [], []
    for d, (q_ref, k_ref, v_ref, gb_ref, gt_ref, o_ref, rev) in enumerate(dirs):
        gb = gb_ref[...]
        for hh in range(heads):
            col = d * n_heads + h0 + hh
            gcol.append(jnp.sum(jnp.where(lane == col, gb, 0.0), axis=1, keepdims=True))
            bcol.append(jnp.sum(jnp.where(lane == col + N_DIRS * n_heads, gb, 0.0), axis=1, keepdims=True))
            grow.append(gt_ref[pl.ds(col, 1), :])
            q.append(q_ref[hh])
            k.append(k_ref[hh])
            v.append(v_ref[hh])
            s.append(s_ref[hh, d])
            reverse.append(rev)
            dest.append((o_ref, hh, d))
    o, s_new = _delta_blocks(q, k, v, gcol, bcol, grow, s, reverse)
    for (o_ref, hh, d), oo, ss in zip(dest, o, s_new):
        o_ref[hh] = oo.astype(o_ref.dtype)
        s_ref[hh, d] = ss

    @pl.when(n == pl.num_programs(2) - 1)
    def _():
        sout_ref[...] = s_ref[...]


def _gdn_core(qkvg, gb, gbt, s0, n_seq, seq_len, n_heads, o_dtype):
    rows = n_seq * seq_len
    c = SCAN_BLOCK
    assert seq_len % c == 0
    nc = seq_len // c
    g = min(CORE_HEADS, n_heads)
    assert n_heads % g == 0
    ng = n_heads // g

    def fwd(b, n):
        return b * nc + n

    def bwd(b, n):
        return b * nc + nc - 1 - n

    def head_spec(part, blk):
        return pl.BlockSpec((g, c, HEAD_DIM), lambda b, hg, n: (part * ng + hg, blk(b, n), 0))

    state_spec = pl.BlockSpec((None, g, N_DIRS, HEAD_DIM, HEAD_DIM), lambda b, hg, n: (b, hg, 0, 0, 0))
    kern = functools.partial(_gdn_core_kernel, n_heads=n_heads)
    return pl.pallas_call(
        kern,
        grid=(n_seq, ng, nc),
        in_specs=[head_spec(0, fwd), head_spec(1, fwd), head_spec(2, fwd),
                  head_spec(0, bwd), head_spec(1, bwd), head_spec(2, bwd),
                  pl.BlockSpec((c, LANES), lambda b, hg, n: (fwd(b, n), 0)),
                  pl.BlockSpec((c, LANES), lambda b, hg, n: (bwd(b, n), 0)),
                  pl.BlockSpec((LANES, c), lambda b, hg, n: (0, fwd(b, n))),
                  pl.BlockSpec((LANES, c), lambda b, hg, n: (0, bwd(b, n))),
                  state_spec],
        out_specs=[pl.BlockSpec((g, c, HEAD_DIM), lambda b, hg, n: (hg, fwd(b, n), 0)),
                   pl.BlockSpec((g, c, HEAD_DIM), lambda b, hg, n: (hg, bwd(b, n), 0)),
                   state_spec],
        out_shape=[jax.ShapeDtypeStruct((n_heads, rows, HEAD_DIM), o_dtype),
                   jax.ShapeDtypeStruct((n_heads, rows, HEAD_DIM), o_dtype),
                   jax.ShapeDtypeStruct(s0.shape, F32)],
        scratch_shapes=[pltpu.VMEM((g, N_DIRS, HEAD_DIM, HEAD_DIM), F32)],
        compiler_params=_params("parallel", "parallel", "arbitrary"),
        name="gdn_core",
    )(qkvg, qkvg, qkvg, qkvg, qkvg, qkvg, gb, gb, gbt, gbt, s0)


def _gdn_combine_kernel(of_ref, ob_ref, gate_ref, g_ref, o_ref):
    for hh in range(of_ref.shape[0]):
        o = of_ref[hh].astype(F32) + ob_ref[hh].astype(F32)
        on = _rmsnorm(o, g_ref[...])
        o_ref[:, hh * HEAD_DIM:(hh + 1) * HEAD_DIM] = (on * _silu(gate_ref[hh].astype(F32))).astype(o_ref.dtype)


def _gdn_combine(o_f, o_b, qkvg, norm_g, tm):
    n_heads, rows, _ = o_f.shape
    g = min(CORE_HEADS, n_heads)
    ng = n_heads // g
    return pl.pallas_call(
        _gdn_combine_kernel,
        grid=(rows // tm, ng),
        in_specs=[pl.BlockSpec((g, tm, HEAD_DIM), lambda i, j: (j, i, 0)),
                  pl.BlockSpec((g, tm, HEAD_DIM), lambda i, j: (j, i, 0)),
                  pl.BlockSpec((g, tm, HEAD_DIM), lambda i, j: (3 * ng + j, i, 0)),
                  pl.BlockSpec((1, HEAD_DIM), lambda i, j: (0, 0))],
        out_specs=pl.BlockSpec((tm, g * HEAD_DIM), lambda i, j: (i, j)),
        out_shape=jax.ShapeDtypeStruct((rows, n_heads * HEAD_DIM), BF16),
        compiler_params=_params("parallel", "parallel"),
        name="gdn_combine",
    )(o_f, o_b, qkvg, norm_g.reshape(1, HEAD_DIM))


def _mm_out_kernel(*refs, gate_row, next_rows, has_scale, emit_y):
    it = iter(refs)
    lhs_ref, w_ref, y_ref, gate_ref = next(it), next(it), next(it), next(it)
    scale_ref = next(it) if has_scale else None
    gn_ref = next(it)
    modn_ref = next(it) if next_rows is not None else None
    ynew_ref = next(it) if emit_y else None
    hn_ref, row_ref = next(it), next(it)
    j = pl.program_id(1)
    nj, _, tn = row_ref.shape

    acc = jnp.dot(lhs_ref[...], w_ref[...], preferred_element_type=F32)
    if has_scale:
        acc = acc * scale_ref[...]
    y_new = y_ref[...] + gate_ref[gate_row:gate_row + 1, :] * acc
    if emit_y:
        ynew_ref[...] = y_new
    row_ref[j] = y_new

    @pl.when(j == nj - 1)
    def _():
        ss = jnp.zeros((row_ref.shape[1], 1), F32)
        for jj in range(nj):
            yj = row_ref[jj]
            ss = ss + jnp.sum(yj * yj, axis=-1, keepdims=True)
        inv = lax.rsqrt(ss / (nj * tn) + NORM_EPS)
        for jj in range(nj):
            cols = slice(jj * tn, (jj + 1) * tn)
            hj = row_ref[jj] * inv * gn_ref[:, cols]
            if next_rows is not None:
                sh, sc = next_rows
                hj = hj * (1.0 + modn_ref[sc:sc + 1, cols]) + modn_ref[sh:sh + 1, cols]
            hn_ref[:, cols] = hj.astype(hn_ref.dtype)


def _mm_out(lhs, w, w_layer, y, modr, layer, gate_row, st, next_g, next_mod, out_dtype, col_scale=None,
            emit_y=True):
    rows, n = y.shape
    grouped = w.ndim == 4
    k = w.shape[-2]
    group_n = w.shape[-1]
    tn = _col_tile(group_n, WIDE_COL_TILE if grouped else None)
    nj = n // tn
    per_group = group_n // tn
    if grouped:
        lhs_spec = pl.BlockSpec((st.tm, k), lambda i, j: (i, j // per_group))
        w_spec = pl.BlockSpec((None, None, k, tn), lambda i, j: (w_layer, j // per_group, 0, j % per_group))
    else:
        lhs_spec = pl.BlockSpec((st.tm, k), lambda i, j: (i, 0))
        w_spec = pl.BlockSpec((None, k, tn), lambda i, j: (w_layer, 0, j))
    in_specs = [lhs_spec, w_spec,
                pl.BlockSpec((st.tm, tn), lambda i, j: (i, j)),
                st.mod_spec(layer, n, tn)]
    args = [lhs, w, y, modr]
    if col_scale is not None:
        in_specs.append(pl.BlockSpec((1, tn), lambda i, j: (0, j)))
        args.append(col_scale.reshape(1, n))
    in_specs.append(pl.BlockSpec((1, n), lambda i, j: (0, 0)))
    args.append(next_g.reshape(1, n))
    next_rows = None
    if next_mod is not None:
        in_specs.append(st.mod_spec(next_mod[0], n))
        args.append(modr)
        next_rows = (next_mod[1], next_mod[2])
    out_specs, out_shape = [], []
    if emit_y:
        out_specs.append(pl.BlockSpec((st.tm, tn), lambda i, j: (i, j)))
        out_shape.append(jax.ShapeDtypeStruct((rows, n), F32))
    out_specs.append(pl.BlockSpec((st.tm, n), lambda i, j: (i, 0)))
    out_shape.append(jax.ShapeDtypeStruct((rows, n), out_dtype))
    kern = functools.partial(_mm_out_kernel, gate_row=gate_row, next_rows=next_rows,
                             has_scale=col_scale is not None, emit_y=emit_y)
    res = pl.pallas_call(
        kern,
        grid=(st.n_tiles, nj),
        in_specs=in_specs,
        out_specs=out_specs,
        out_shape=out_shape,
        scratch_shapes=[pltpu.VMEM((nj, st.tm, tn), F32)],
        compiler_params=_params("parallel", "arbitrary"),
        name="mm_out",
    )(*args)
    return res if emit_y else (None, res[0])


def kernel(x, c, ctx, c_ctx, mod_w, mod_b, norm1_g, norm2_g, ffn_w_in, ffn_conv, ffn_w_out, pool_w, pool_scale, sc_w_in, sc_conv, sc_w_out, gdn_w_in, gdn_conv, gdn_a_log, gdn_dt_bias, gdn_norm_g, gdn_w_out, final_g):
    b, t, d = x.shape
    t_ctx = ctx.shape[1]
    depth = mod_w.shape[0]
    n_mix = 3
    n_heads = d // HEAD_DIM
    assert 2 * N_DIRS * n_heads <= LANES
    gdn_layers = [i for i in range(depth) if i % n_mix == 2]
    last_ctx = gdn_layers[-1] if gdn_layers else -1

    lat = _Stream(b, t, min(ROW_TILE, t))
    cst = _Stream(b, t_ctx, _ctx_tile(ROW_TILE, b, t_ctx), ctx_row=b)
    lat.proj = _Stream(b, t, min(IN_ROW_TILE, t))
    cst.proj = _Stream(b, t_ctx, _ctx_tile(IN_ROW_TILE, b, t_ctx), ctx_row=b)
    modr = _mod_vectors(c, c_ctx, mod_w, mod_b)

    y = x.reshape(b * t, d)
    yc = ctx.reshape(b * t_ctx, d)

    def h_dtype(i):
        return F32 if i % n_mix == 0 else BF16

    h = _norm_mod(y, modr, 0, norm1_g[0], lat, h_dtype(0))
    hc = _norm_mod(yc, modr, 0, norm1_g[0], cst, h_dtype(0)) if last_ctx >= 0 else None

    ffn_w_in, ffn_w_out, pool_w, sc_w_in, sc_w_out, gdn_w_in, gdn_w_out = (
        w.astype(BF16) for w in (ffn_w_in, ffn_w_out, pool_w, sc_w_in, sc_w_out, gdn_w_in, gdn_w_out))

    out = None
    for i in range(depth):
        kind, jx = i % n_mix, i // n_mix
        ctx_stream = i < last_ctx
        last = i == depth - 1
        streams = [(lat, h, y, t, GRID_W)]
        if ctx_stream:
            streams.append((cst, hc, yc, t_ctx, None))

        if kind == 2:
            ab_par = jnp.pad(jnp.stack([gdn_a_log[jx].reshape(-1), gdn_dt_bias[jx].reshape(-1)]),
                             ((0, 0), (0, LANES - N_DIRS * n_heads)))
            qc, gbc, gbtc = _gdn_in(hc, gdn_w_in, gdn_conv, jx, ab_par, cst.proj, n_heads)
            s0 = jnp.zeros((b, n_heads, N_DIRS, HEAD_DIM, HEAD_DIM), F32)
            ocf, ocb, s_ctx = _gdn_core(qc, gbc, gbtc, s0, b, t_ctx, n_heads, BF16)
            ql, gbl, gbtl = _gdn_in(h, gdn_w_in, gdn_conv, jx, ab_par, lat.proj, n_heads)
            o_f, o_b, _ = _gdn_core(ql, gbl, gbtl, s_ctx, b, t, n_heads, BF16)
            mixed = [(_gdn_combine(o_f, o_b, ql, gdn_norm_g[jx], lat.tm), gdn_w_out, None)]
            if ctx_stream:
                mixed.append((_gdn_combine(ocf, ocb, qc, gdn_norm_g[jx], cst.tm), gdn_w_out, None))
        elif kind == 1:
            mixed = [(_sc_in(hs, sc_w_in, sc_conv, jx, st), sc_w_out, None) for st, hs, _, _, _ in streams]
        else:
            mixed = [(_pool_delta(hs, b, sl, gw), pool_w, pool_scale[jx]) for st, hs, _, sl, gw in streams]

        new = []
        for (st, _, ys, _, _), (m_lhs, w_out, scale) in zip(streams, mixed):
            y1, h2 = _mm_out(m_lhs, w_out, jx, ys, modr, i, GATE1, st, norm2_g[i], (i, SHIFT2, SCALE2), BF16,
                             col_scale=scale)
            f = _ffn_in(h2, ffn_w_in, ffn_conv, i, st.proj)
            if last:
                _, hn = _mm_out(f, ffn_w_out, i, y1, modr, i, GATE2, st, final_g, None, F32, emit_y=False)
                y2 = None
            else:
                y2, hn = _mm_out(f, ffn_w_out, i, y1, modr, i, GATE2, st, norm1_g[i + 1],
                                 (i + 1, SHIFT1, SCALE1), h_dtype(i + 1))
            new.append((y2, hn))
        y, h = new[0]
        if ctx_stream:
            yc, hc = new[1]
        if last:
            out = h
    return out.reshape(b, t, d).astype(x.dtype)
```
